```python
import jax
import jax.numpy as jnp
from jax import lax
import numpy as np

D_MODEL = 4096
BATCH = 4
SEQ = 4096
DEPTH = 1
DEC_BATCH = 1
DEC_SEQ = 16384
PAST_LEN = 128

GRID_W = 64
MIX_WIDTH = D_MODEL
MLSTM_HEADS = 8
MLSTM_WIDTH = MIX_WIDTH // 2
MLSTM_V_DIM = MLSTM_WIDTH // MLSTM_HEADS
MLSTM_QK_DIM = MLSTM_V_DIM // 2
MLSTM_CHUNK = 64
GATE_SOFTCAP = 15.0
FORGET_BIAS = 3.0
NA_HEADS = 16
NA_WIDTH = MIX_WIDTH - MLSTM_WIDTH
NA_HEAD_DIM = NA_WIDTH // NA_HEADS
NA_WIN_ROWS = 8
NA_WIN_COLS = 16
N_EXPERTS = 32
TOP_K = 4
D_FF = D_MODEL
SWIGLU_LIMIT = 7.0
SWIGLU_ALPHA = 1.702
EXPERT_BLOCK = 256
RMS_EPS = 1e-6
IN_WIDTHS = (MLSTM_HEADS * MLSTM_QK_DIM, MLSTM_HEADS * MLSTM_QK_DIM, MLSTM_WIDTH, MLSTM_WIDTH, 4 * MLSTM_HEADS, NA_WIDTH, NA_WIDTH, NA_WIDTH)
IN_WIDTH = sum(IN_WIDTHS)

kernel_name = 'hybrid_mlstm_natten_moe_encoder'


def _split_offsets():
    offs, acc = [], 0
    for w in IN_WIDTHS[:-1]:
        acc += w
        offs.append(acc)
    return offs


def rmsnorm(x, g):
    xf = x.astype(jnp.float32)
    y = xf * lax.rsqrt(jnp.mean(xf * xf, axis=-1, keepdims=True) + RMS_EPS)
    return (y * g.astype(jnp.float32)).astype(x.dtype)


def head_rmsnorm(h, g):
    hf = h.astype(jnp.float32)
    y = hf * lax.rsqrt(jnp.mean(hf * hf, axis=-1, keepdims=True) + RMS_EPS)
    return y * g.astype(jnp.float32).reshape(h.shape[-2], h.shape[-1])


def mlstm_chunkwise(q, k, v, ig, fg):
    B, T, H, dk = q.shape
    dv = v.shape[-1]
    L = MLSTM_CHUNK
    nc = T // L
    f32 = jnp.float32

    def to_chunks(a):
        return a.astype(f32).reshape(B, nc, L, H, a.shape[-1]).transpose(1, 0, 3, 2, 4)

    def gate_chunks(a):
        return a.reshape(B, nc, L, H).transpose(1, 0, 3, 2)

    qc = to_chunks(q)
    kc = to_chunks(k) * (dk ** -0.5)
    vc = to_chunks(v)
    ic = gate_chunks(ig)
    bc = jnp.cumsum(gate_chunks(jax.nn.log_sigmoid(fg)), axis=-1)
    causal = jnp.tril(jnp.ones((L, L), dtype=bool))

    def step(carry, xs):
        C, n, m = carry
        q_, k_, v_, i_, b_ = xs
        log_d = jnp.where(causal, b_[..., :, None] - b_[..., None, :] + i_[..., None, :], -jnp.inf)
        log_inter = b_ + m[..., None]
        m_q = jnp.maximum(log_inter, jnp.max(log_d, axis=-1))
        s = jnp.einsum('bhjd,bhsd->bhjs', q_, k_) * jnp.exp(log_d - m_q[..., None])
        inter = jnp.exp(log_inter - m_q)
        num = jnp.einsum('bhjs,bhsv->bhjv', s, v_) + inter[..., None] * jnp.einsum('bhjd,bhvd->bhjv', q_, C)
        den = jnp.sum(s, axis=-1) + inter * jnp.einsum('bhjd,bhd->bhj', q_, n)
        h = num / jnp.maximum(jnp.abs(den), jnp.exp(-m_q))[..., None]
        b_last = b_[..., -1]
        log_w = b_last[..., None] - b_ + i_
        m_new = jnp.maximum(b_last + m, jnp.max(log_w, axis=-1))
        w = jnp.exp(log_w - m_new[..., None])
        decay = jnp.exp(b_last + m - m_new)
        C_new = decay[..., None, None] * C + jnp.einsum('bhs,bhsv,bhsd->bhvd', w, v_, k_)
        n_new = decay[..., None] * n + jnp.einsum('bhs,bhsd->bhd', w, k_)
        return (C_new, n_new, m_new), h

    init = (jnp.zeros((B, H, dv, dk), f32), jnp.zeros((B, H, dk), f32), jnp.zeros((B, H), f32))
    _, hs = lax.scan(step, init, (qc, kc, vc, ic, bc))
    return hs.transpose(1, 0, 3, 2, 4).reshape(B, T, H, dv)


def mlstm_bidirectional(q, k, v, gates):
    ig_f, fg_f, ig_b, fg_b = jnp.split(gates, 4, axis=-1)
    flip = lambda a: jnp.flip(a, axis=1)
    h_fwd = mlstm_chunkwise(q, k, v, ig_f, fg_f)
    h_bwd = flip(mlstm_chunkwise(flip(q), flip(k), flip(v), flip(ig_b), flip(fg_b)))
    return h_fwd + h_bwd


def neighborhood_attention(q, k, v, rpb):
    B, T, H, d = q.shape
    rows = T // GRID_W
    kh = min(NA_WIN_ROWS, rows)
    kw = NA_WIN_COLS
    qg = q.reshape(B, rows, GRID_W, H, d) * (d ** -0.5)
    kg = k.reshape(B, rows, GRID_W, H, d)
    vg = v.reshape(B, rows, GRID_W, H, d)
    row_start = jnp.clip(jnp.arange(rows) - kh // 2, 0, rows - kh)
    col_start = jnp.clip(jnp.arange(GRID_W) - kw // 2, 0, GRID_W - kw)
    col_idx = col_start[:, None] + jnp.arange(kw)[None, :]
    col_off = col_idx - jnp.arange(GRID_W)[:, None] + (NA_WIN_COLS - 1)
    rpb_cols = rpb.astype(jnp.float32)[:, :, col_off]

    def row_fn(args):
        r, q_row = args
        rs = row_start[r]
        k_nb = lax.dynamic_slice_in_dim(kg, rs, kh, axis=1)[:, :, col_idx]
        v_nb = lax.dynamic_slice_in_dim(vg, rs, kh, axis=1)[:, :, col_idx]
        row_off = rs + jnp.arange(kh) - r + (NA_WIN_ROWS - 1)
        bias = rpb_cols[:, row_off].transpose(0, 2, 1, 3)
        logits = jnp.einsum('bqhd,biqjhd->bhqij', q_row, k_nb).astype(jnp.float32) + bias[None]
        p = jax.nn.softmax(logits.reshape(B, H, GRID_W, kh * kw), axis=-1)
        p = p.reshape(B, H, GRID_W, kh, kw).astype(v.dtype)
        return jnp.einsum('bhqij,biqjhd->bqhd', p, v_nb)

    out = lax.map(row_fn, (jnp.arange(rows), qg.transpose(1, 0, 2, 3, 4)))
    return out.transpose(1, 0, 2, 3, 4).reshape(B, T, H, d)


def moe(h, w_router, b_router, w_gate_up, b_gate_up, w_down, b_down):
    B, T, D = h.shape
    N = B * T
    NK = N * TOP_K
    xt = h.reshape(N, D)
    logits = xt.astype(jnp.float32) @ w_router.astype(jnp.float32) + b_router.astype(jnp.float32)
    top_val, top_idx = lax.top_k(logits, TOP_K)
    gate = jax.nn.softmax(top_val, axis=-1)
    flat_e = top_idx.reshape(-1).astype(jnp.int32)
    flat_tok = jnp.repeat(jnp.arange(N, dtype=jnp.int32), TOP_K)
    order = jnp.argsort(flat_e)
    se, stok, sgate = flat_e[order], flat_tok[order], gate.reshape(-1)[order]
    counts = jnp.bincount(flat_e, length=N_EXPERTS)
    padded = (counts + EXPERT_BLOCK - 1) // EXPERT_BLOCK * EXPERT_BLOCK
    start = jnp.cumsum(counts) - counts
    pend = jnp.cumsum(padded)
    pstart = pend - padded
    dest = pstart[se] + jnp.arange(NK, dtype=jnp.int32) - start[se]
    nb = -(-NK // EXPERT_BLOCK) + N_EXPERTS
    P = nb * EXPERT_BLOCK
    slot_tok = jnp.full((P,), N, dtype=jnp.int32).at[dest].set(stok)
    slot_gate = jnp.zeros((P,), jnp.float32).at[dest].set(sgate)
    block_expert = jnp.minimum(jnp.searchsorted(pend, jnp.arange(nb) * EXPERT_BLOCK, side='right'), N_EXPERTS - 1)
    x_pad = jnp.concatenate([xt, jnp.zeros((1, D), xt.dtype)], axis=0)
    xb = x_pad[slot_tok].reshape(nb, EXPERT_BLOCK, D)

    def expert_fn(args):
        x_blk, e = args
        gu = x_blk @ w_gate_up[e] + b_gate_up[e]
        g_ = jnp.minimum(gu[:, :D_FF], SWIGLU_LIMIT)
        u_ = jnp.clip(gu[:, D_FF:], -SWIGLU_LIMIT, SWIGLU_LIMIT)
        glu = g_ * jax.nn.sigmoid(g_ * SWIGLU_ALPHA)
        return ((u_ + 1.0) * glu) @ w_down[e] + b_down[e]

    yb = lax.map(expert_fn, (xb, block_expert))
    y = yb.reshape(P, D).astype(jnp.float32) * slot_gate[:, None]
    out = jnp.zeros((N + 1, D), jnp.float32).at[slot_tok].add(y)[:N]
    return out.reshape(B, T, D).astype(h.dtype)


def trunk_layer(x, g_mix, w_in, b_gates, g_mlstm_out, g_na_out, rpb, w_out, g_ffn, w_router, b_router, w_gate_up, b_gate_up, w_down, b_down):
    B, T, _ = x.shape
    h = rmsnorm(x, g_mix)
    proj = h @ w_in
    qm, km, vm, om, gm, qa, ka, va = jnp.split(proj, _split_offsets(), axis=-1)
    gates = gm.astype(jnp.float32) + b_gates.astype(jnp.float32)
    gates = GATE_SOFTCAP * jnp.tanh(gates / GATE_SOFTCAP)
    hm = mlstm_bidirectional(qm.reshape(B, T, MLSTM_HEADS, MLSTM_QK_DIM), km.reshape(B, T, MLSTM_HEADS, MLSTM_QK_DIM), vm.reshape(B, T, MLSTM_HEADS, MLSTM_V_DIM), gates)
    o_gate = jax.nn.sigmoid(om.astype(jnp.float32)).reshape(B, T, MLSTM_HEADS, MLSTM_V_DIM)
    hm = (head_rmsnorm(hm, g_mlstm_out) * o_gate).reshape(B, T, MLSTM_WIDTH).astype(x.dtype)
    ha = neighborhood_attention(qa.reshape(B, T, NA_HEADS, NA_HEAD_DIM), ka.reshape(B, T, NA_HEADS, NA_HEAD_DIM), va.reshape(B, T, NA_HEADS, NA_HEAD_DIM), rpb)
    ha = head_rmsnorm(ha, g_na_out).reshape(B, T, NA_WIDTH).astype(x.dtype)
    x = x + jnp.concatenate([hm, ha], axis=-1) @ w_out
    x = x + moe(rmsnorm(x, g_ffn), w_router, b_router, w_gate_up, b_gate_up, w_down, b_down)
    return x


def setup_inputs(seed: int = 0) -> dict:
    key = jax.random.key(seed)
    ks = jax.random.split(key, 20)
    f32 = jnp.float32
    nrm = lambda k, shape, s: jax.random.normal(k, shape, f32) * s
    gate_offset = jnp.array([0.0, FORGET_BIAS, 0.0, FORGET_BIAS], f32)[None, :, None]
    return {
        'x_prompt': nrm(ks[0], (BATCH, SEQ, D_MODEL), 1.0),
        'x_sample': nrm(ks[1], (DEC_BATCH, DEC_SEQ, D_MODEL), 1.0),
        'g_mix': 1.0 + nrm(ks[2], (DEPTH, D_MODEL), 0.05),
        'w_in': nrm(ks[3], (DEPTH, D_MODEL, IN_WIDTH), D_MODEL ** -0.5),
        'b_gates': (gate_offset + nrm(ks[4], (DEPTH, 4, MLSTM_HEADS), 0.1)).reshape(DEPTH, 4 * MLSTM_HEADS),
        'g_mlstm_out': 1.0 + nrm(ks[5], (DEPTH, MLSTM_WIDTH), 0.05),
        'g_na_out': 1.0 + nrm(ks[6], (DEPTH, NA_WIDTH), 0.05),
        'rpb': nrm(ks[7], (DEPTH, NA_HEADS, 2 * NA_WIN_ROWS - 1, 2 * NA_WIN_COLS - 1), 0.1),
        'w_out': nrm(ks[8], (DEPTH, MIX_WIDTH, D_MODEL), MIX_WIDTH ** -0.5),
        'g_ffn': 1.0 + nrm(ks[9], (DEPTH, D_MODEL), 0.05),
        'w_router': nrm(ks[10], (DEPTH, D_MODEL, N_EXPERTS), D_MODEL ** -0.5),
        'b_router': nrm(ks[11], (DEPTH, N_EXPERTS), 0.01),
        'w_gate_up': nrm(ks[12], (DEPTH, N_EXPERTS, D_MODEL, 2 * D_FF), D_MODEL ** -0.5),
        'b_gate_up': nrm(ks[13], (DEPTH, N_EXPERTS, 2 * D_FF), 0.01),
        'w_down': nrm(ks[14], (DEPTH, N_EXPERTS, D_FF, D_MODEL), D_FF ** -0.5),
        'b_down': nrm(ks[15], (DEPTH, N_EXPERTS, D_MODEL), 0.01),
        'g_final': 1.0 + nrm(ks[16], (D_MODEL,), 0.05),
    }


def reference(x_prompt, x_sample, g_mix, w_in, b_gates, g_mlstm_out, g_na_out, rpb, w_out, g_ffn, w_router, b_router, w_gate_up, b_gate_up, w_down, b_down, g_final):
    def run(x):
        for l in range(DEPTH):
            x = trunk_layer(x, g_mix[l], w_in[l], b_gates[l], g_mlstm_out[l], g_na_out[l], rpb[l], w_out[l], g_ffn[l], w_router[l], b_router[l], w_gate_up[l], b_gate_up[l], w_down[l], b_down[l])
        return rmsnorm(x, g_final)
    y_prompt = run(x_prompt)
    y_sample = run(x_sample)
    return (y_prompt, y_sample)
```

```python
import functools

import jax
import jax.numpy as jnp
from jax import lax
from jax.experimental import pallas as pl
from jax.experimental.pallas import tpu as pltpu

GRID_W = 64
GATE_SOFTCAP = 15.0
NA_WIN_ROWS = 8
NA_WIN_COLS = 16
TOP_K = 4
SWIGLU_LIMIT = 7.0
SWIGLU_ALPHA = 1.702
RMS_EPS = 1e-6

LANES = 128
MLSTM_CHUNK = 256
NEG_BIG = -1e30
MIB = 1024 * 1024

F32 = jnp.float32
BF16 = jnp.bfloat16


def _params(vmem_mib, n_axes):
    return pltpu.CompilerParams(
        dimension_semantics=("arbitrary",) * n_axes, vmem_limit_bytes=vmem_mib * MIB)


def _dot(a, b):
    return jnp.dot(a, b, preferred_element_type=F32)


def _dot_nt(a, b):
    return lax.dot_general(a, b, (((1,), (1,)), ((), ())), preferred_element_type=F32)


def _dot_tn(a, b):
    return lax.dot_general(a, b, (((0,), (0,)), ((), ())), preferred_element_type=F32)


def _split3(x):
    hi = x.astype(BF16)
    r1 = x - hi.astype(F32)
    mid = r1.astype(BF16)
    lo = (r1 - mid.astype(F32)).astype(BF16)
    return hi, mid, lo


def _rms(x, g):
    return x * lax.rsqrt(jnp.mean(x * x, axis=-1, keepdims=True) + RMS_EPS) * g


def _norm_in_kernel(nbp, xp_ref, xs_ref, g_ref, o_ref):
    i = pl.program_id(0)
    x = jnp.where(i < nbp, xp_ref[...], xs_ref[...])
    o_ref[...] = _rms(x, g_ref[...]).astype(o_ref.dtype)


def _norm_in(xp, xs, g, tm=256):
    np_, d = xp.shape
    ns = xs.shape[0]
    nbp, nbs = np_ // tm, ns // tm
    return pl.pallas_call(
        functools.partial(_norm_in_kernel, nbp),
        out_shape=jax.ShapeDtypeStruct((np_ + ns, d), BF16),
        grid=(nbp + nbs,),
        in_specs=[
            pl.BlockSpec((tm, d), lambda i: (jnp.minimum(i, nbp - 1), 0)),
            pl.BlockSpec((tm, d), lambda i: (jnp.maximum(i - nbp, 0), 0)),
            pl.BlockSpec((1, d), lambda i: (0, 0)),
        ],
        out_specs=pl.BlockSpec((tm, d), lambda i: (i, 0)),
        compiler_params=_params(48, 1),
        name="norm_in",
    )(xp, xs, g)


def _matmul_kernel(a_ref, b_ref, o_ref):
    o_ref[...] = _dot(a_ref[...], b_ref[...]).astype(o_ref.dtype)


def _matmul(a, b, tm=1024, tn=1024):
    m, k = a.shape
    n = b.shape[1]
    tm, tn = min(tm, m), min(tn, n)
    return pl.pallas_call(
        _matmul_kernel,
        out_shape=jax.ShapeDtypeStruct((m, n), BF16),
        grid=(m // tm, n // tn),
        in_specs=[pl.BlockSpec((tm, k), lambda i, j: (i, 0)),
                  pl.BlockSpec((k, tn), lambda i, j: (0, j))],
        out_specs=pl.BlockSpec((tm, tn), lambda i, j: (i, j)),
        compiler_params=_params(48, 2),
        name="in_proj",
    )(a, b)


def _softcap(x):
    return GATE_SOFTCAP * jnp.tanh(x / GATE_SOFTCAP)


def _log_sigmoid(x):
    return jnp.minimum(x, 0.0) - jnp.log1p(jnp.exp(-jnp.abs(x)))


def _gates_kernel(L, h_ref, wi_ref, wf_ref, bi_ref, bf_ref, wit_ref, wft_ref, bit_ref, bft_ref,
                  col_ref, row_ref):
    h = h_ref[...]
    t = h.shape[0]
    gi = _softcap(_dot(h, wi_ref[...]) + bi_ref[...])
    lf = _log_sigmoid(_softcap(_dot(h, wf_ref[...]) + bf_ref[...]))
    git = _softcap(_dot_nt(wit_ref[...], h) + bit_ref[...])
    lft = _log_sigmoid(_softcap(_dot_nt(wft_ref[...], h) + bft_ref[...]))

    r = lax.broadcasted_iota(jnp.int32, (L, L), 0)
    c = lax.broadcasted_iota(jnp.int32, (L, L), 1)
    tri = (c <= r).astype(BF16)
    trit = (r <= c).astype(BF16)

    lane = lax.broadcasted_iota(jnp.int32, (L, LANES), 1)
    q = lane % 16
    is_bwd = q >= 3
    kind = jnp.where(is_bwd, q - 3, q)
    rowi = lax.broadcasted_iota(jnp.int32, (16, L), 0)
    row_bwd = (rowi % 2) == 1

    for ci in range(t // L):
        sl = slice(ci * L, (ci + 1) * L)
        lfc, gic = lf[sl], gi[sl]
        hi, mid, lo = _split3(lfc)
        b = _dot(tri, hi) + _dot(tri, mid) + _dot(tri, lo)
        a = b - lfc
        tot = b[L - 1:L, :]
        alpha = jnp.where(is_bwd, -a, b)
        gamma = jnp.where(is_bwd, tot - a, b)
        omega = jnp.where(is_bwd, a + gic, tot - b + gic)
        col_ref[sl, :] = jnp.where(kind == 0, alpha, jnp.where(kind == 1, gamma, omega))

        lfct, gict = lft[:, sl], git[:, sl]
        hi, mid, lo = _split3(lfct)
        bt = _dot(hi, trit) + _dot(mid, trit) + _dot(lo, trit)
        at = bt - lfct
        row_ref[:, sl] = jnp.where(row_bwd, at + gict, gict - bt)


def _gates(h, wi, wf, bi, bf, wit, wft, bit, bft, L, tm=1024):
    n, d = h.shape
    tm = min(tm, n)
    full = lambda i: (0, 0)
    return pl.pallas_call(
        functools.partial(_gates_kernel, L),
        out_shape=(jax.ShapeDtypeStruct((n, LANES), F32), jax.ShapeDtypeStruct((16, n), F32)),
        grid=(n // tm,),
        in_specs=[pl.BlockSpec((tm, d), lambda i: (i, 0)),
                  pl.BlockSpec((d, LANES), full), pl.BlockSpec((d, LANES), full),
                  pl.BlockSpec((1, LANES), full), pl.BlockSpec((1, LANES), full),
                  pl.BlockSpec((16, d), full), pl.BlockSpec((16, d), full),
                  pl.BlockSpec((16, 1), full), pl.BlockSpec((16, 1), full)],
        out_specs=(pl.BlockSpec((tm, LANES), lambda i: (i, 0)),
                   pl.BlockSpec((16, tm), lambda i: (0, i))),
        compiler_params=_params(40, 1),
        name="mlstm_gates",
    )(h, wi, wf, bi, bf, wit, wft, bit, bft)


def _mlstm_kernel(H, L, dk, dv, fwd_ref, bwd_ref, reset_ref,
                  qf_ref, kf_ref, vf_ref, colf_ref, rowf_ref,
                  qb_ref, kb_ref, vb_ref, colb_ref, rowb_ref,
                  hf_ref, hb_ref, ct_ref, nr_ref):
    s = pl.program_id(0)

    @pl.when(reset_ref[s] == 1)
    def _():
        ct_ref[...] = jnp.zeros_like(ct_ref)
        nr_ref[...] = jnp.zeros_like(nr_ref)

    r = lax.broadcasted_iota(jnp.int32, (L, L), 0)
    c = lax.broadcasted_iota(jnp.int32, (L, L), 1)
    masks = (c <= r, c >= r)
    ones = jnp.ones((L, LANES), BF16)
    scale = dk ** -0.5
    dirs = ((qf_ref, kf_ref, vf_ref, colf_ref, rowf_ref, hf_ref),
            (qb_ref, kb_ref, vb_ref, colb_ref, rowb_ref, hb_ref))

    for hd in range(H):
        for d, (q_ref, k_ref, v_ref, col_ref, row_ref, o_ref) in enumerate(dirs):
            ch = hd * 2 + d
            q = q_ref[:, hd * dk:(hd + 1) * dk]
            k = k_ref[:, hd * dk:(hd + 1) * dk]
            v = v_ref[:, hd * dv:(hd + 1) * dv]
            lane0 = hd * 16 + d * 3
            alpha = col_ref[:, lane0:lane0 + 1]
            gamma = col_ref[:, lane0 + 1:lane0 + 2]
            omega = col_ref[:, lane0 + 2:lane0 + 3]
            beta = row_ref[ch:ch + 1, :]
            g_end = gamma[L - 1:L, :] if d == 0 else gamma[0:1, :]

            logd = jnp.where(masks[d], alpha + beta, -jnp.inf)
            p = (_dot_nt(q, k) * scale) * jnp.exp(logd)
            pb = p.astype(BF16)
            ct = ct_ref[ch]
            nr = nr_ref[ch]
            eg = jnp.exp(gamma) * scale
            num = _dot(pb, v) + eg * _dot(q, ct.astype(BF16))
            den = _dot(pb, ones) + eg * _dot(q, nr.astype(BF16))
            inv = 1.0 / jnp.maximum(jnp.abs(den), 1.0)
            o_ref[:, hd * dv:(hd + 1) * dv] = (
                num * jnp.concatenate([inv] * (dv // LANES), axis=1)).astype(o_ref.dtype)

            kw = (k.astype(F32) * jnp.exp(omega)).astype(BF16)
            decay = jnp.exp(g_end)
            ct_ref[ch] = decay * ct + _dot_tn(kw, v)
            nr_ref[ch] = decay * nr + _dot_tn(kw, ones)


def _mlstm(proj, col, row, sched, H, L, dk, dv, q_off, k_off, v_off):
    n = proj.shape[0]
    fwd, bwd, reset = sched
    steps = fwd.shape[0]
    qw, vw = H * dk, H * dv
    qb, kb, vb = q_off // qw, k_off // qw, v_off // vw
    fi = lambda s, f, b, r: f[s]
    bi = lambda s, f, b, r: b[s]

    def specs(pick):
        return [pl.BlockSpec((L, qw), lambda s, f, b, r: (pick(s, f, b, r), qb)),
                pl.BlockSpec((L, qw), lambda s, f, b, r: (pick(s, f, b, r), kb)),
                pl.BlockSpec((L, vw), lambda s, f, b, r: (pick(s, f, b, r), vb)),
                pl.BlockSpec((L, LANES), lambda s, f, b, r: (pick(s, f, b, r), 0)),
                pl.BlockSpec((16, L), lambda s, f, b, r: (0, pick(s, f, b, r)))]

    grid_spec = pltpu.PrefetchScalarGridSpec(
        num_scalar_prefetch=3, grid=(steps,),
        in_specs=specs(fi) + specs(bi),
        out_specs=(pl.BlockSpec((L, vw), lambda s, f, b, r: (f[s], 0)),
                   pl.BlockSpec((L, vw), lambda s, f, b, r: (b[s], 0))),
        scratch_shapes=[pltpu.VMEM((2 * H, dk, dv), F32), pltpu.VMEM((2 * H, dk, LANES), F32)])
    return pl.pallas_call(
        functools.partial(_mlstm_kernel, H, L, dk, dv),
        out_shape=(jax.ShapeDtypeStruct((n, vw), BF16), jax.ShapeDtypeStruct((n, vw), BF16)),
        grid_spec=grid_spec,
        compiler_params=_params(40, 1),
        name="mlstm",
    )(fwd, bwd, reset, proj, proj, proj, col, row, proj, proj, proj, col, row)


def _na_kernel(rows, d, q_ref, k_ref, v_ref, bias_ref, g_ref, o_ref):
    scale = d ** -0.5
    win = NA_WIN_ROWS * GRID_W
    g = g_ref[...]

    def block(rb, carry):
        for u in range(NA_WIN_ROWS):
            r = rb * NA_WIN_ROWS + u
            rs = jnp.clip(r - NA_WIN_ROWS // 2, 0, rows - NA_WIN_ROWS)
            q = q_ref[pl.ds(pl.multiple_of(r * GRID_W, GRID_W), GRID_W), :]
            ks = pl.multiple_of(rs * GRID_W, GRID_W)
            kw = k_ref[pl.ds(ks, win), :]
            vw = v_ref[pl.ds(ks, win), :]
            s = _dot_nt(q, kw) * scale + bias_ref[r - rs]
            m = jnp.max(s, axis=-1, keepdims=True)
            p = jnp.exp(s - m)
            l = jnp.sum(p, axis=-1, keepdims=True)
            o = _dot(p.astype(BF16), vw) / l
            o_ref[pl.ds(pl.multiple_of(r * GRID_W, GRID_W), GRID_W), :] = _rms(o, g).astype(o_ref.dtype)
        return carry

    lax.fori_loop(0, rows // NA_WIN_ROWS, block, 0)


def _na(proj, bias, g, nseq, row0, T, heads, d, q_off, k_off, v_off):
    rows = T // GRID_W
    sb = row0 // T
    qb, kb, vb = q_off // d, k_off // d, v_off // d
    return pl.pallas_call(
        functools.partial(_na_kernel, rows, d),
        out_shape=jax.ShapeDtypeStruct((nseq * T, heads * d), BF16),
        grid=(nseq, heads),
        in_specs=[pl.BlockSpec((T, d), lambda b, h: (sb + b, qb + h)),
                  pl.BlockSpec((T, d), lambda b, h: (sb + b, kb + h)),
                  pl.BlockSpec((T, d), lambda b, h: (sb + b, vb + h)),
                  pl.BlockSpec((None, NA_WIN_ROWS, GRID_W, NA_WIN_ROWS * GRID_W), lambda b, h: (h, 0, 0, 0)),
                  pl.BlockSpec((1, d), lambda b, h: (0, h))],
        out_specs=pl.BlockSpec((T, d), lambda b, h: (b, h)),
        compiler_params=_params(48, 2),
        name="natten",
    )(proj, proj, proj, bias, g)


def _na_bias(rpb):
    heads = rpb.shape[0]
    cq = jnp.arange(GRID_W)
    cs = jnp.clip(cq - NA_WIN_COLS // 2, 0, GRID_W - NA_WIN_COLS)
    j = jnp.arange(GRID_W)
    inside = (j[None, :] >= cs[:, None]) & (j[None, :] < cs[:, None] + NA_WIN_COLS)
    coff = jnp.clip(j[None, :] - cq[:, None] + (NA_WIN_COLS - 1), 0, 2 * NA_WIN_COLS - 2)
    var = jnp.arange(NA_WIN_ROWS)
    i = jnp.arange(NA_WIN_ROWS)
    roff = i[None, :] - var[:, None] + (NA_WIN_ROWS - 1)
    tab = rpb.astype(F32)[:, roff[:, :, None, None], coff[None, None, :, :]]
    tab = jnp.where(inside[None, None, None], tab, NEG_BIG)
    tab = tab.transpose(0, 1, 3, 2, 4)
    return tab.reshape(heads, NA_WIN_ROWS, GRID_W, NA_WIN_ROWS * GRID_W)


def _out_proj_kernel(nbp, H, dv, hf_ref, hb_ref, om_ref, hap_ref, has_ref, g_ref, w_ref,
                     xp_ref, xs_ref, o_ref, lhs_ref):
    i = pl.program_id(0)
    j = pl.program_id(1)
    wm = H * dv

    @pl.when(j == 0)
    def _():
        for hd in range(H):
            sl = slice(hd * dv, (hd + 1) * dv)
            hs = hf_ref[:, sl].astype(F32) + hb_ref[:, sl].astype(F32)
            y = _rms(hs, g_ref[:, sl]) * jax.nn.sigmoid(om_ref[:, sl].astype(F32))
            lhs_ref[:, sl] = y.astype(BF16)
        lhs_ref[:, wm:] = jnp.where(i < nbp, hap_ref[...], has_ref[...])

    x = jnp.where(i < nbp, xp_ref[...], xs_ref[...])
    o_ref[...] = x + _dot(lhs_ref[...], w_ref[...])


def _out_proj(hf, hb, proj, om_off, hap, has_, g_ml, w_out, xp, xs, H, dv, tm=512, tn=512):
    n, wm = hf.shape
    wa = hap.shape[1]
    d = w_out.shape[1]
    tm, tn = min(tm, xp.shape[0], xs.shape[0]), min(tn, d)
    nbp = xp.shape[0] // tm
    omb = om_off // wm
    pidx = lambda i, j: (jnp.minimum(i, nbp - 1), 0)
    sidx = lambda i, j: (jnp.maximum(i - nbp, 0), 0)
    return pl.pallas_call(
        functools.partial(_out_proj_kernel, nbp, H, dv),
        out_shape=jax.ShapeDtypeStruct((n, d), F32),
        grid=(n // tm, d // tn),
        in_specs=[pl.BlockSpec((tm, wm), lambda i, j: (i, 0)),
                  pl.BlockSpec((tm, wm), lambda i, j: (i, 0)),
                  pl.BlockSpec((tm, wm), lambda i, j: (i, omb)),
                  pl.BlockSpec((tm, wa), pidx),
                  pl.BlockSpec((tm, wa), sidx),
                  pl.BlockSpec((1, wm), lambda i, j: (0, 0)),
                  pl.BlockSpec((wm + wa, tn), lambda i, j: (0, j)),
                  pl.BlockSpec((tm, tn), lambda i, j: (jnp.minimum(i, nbp - 1), j)),
                  pl.BlockSpec((tm, tn), lambda i, j: (jnp.maximum(i - nbp, 0), j))],
        out_specs=pl.BlockSpec((tm, tn), lambda i, j: (i, j)),
        scratch_shapes=[pltpu.VMEM((tm, wm + wa), BF16)],
        compiler_params=_params(48, 2),
        name="out_proj",
    )(hf, hb, proj, hap, has_, g_ml, w_out, xp, xs)


def _router_kernel(E, x_ref, g_ref, whi_ref, wlo_ref, b_ref, idx_ref, gate_ref, rank_ref, cnt_ref,
                   carry_ref):
    i = pl.program_id(0)

    @pl.when(i == 0)
    def _():
        carry_ref[...] = jnp.zeros_like(carry_ref)

    h = _rms(x_ref[...], g_ref[...])
    tm = h.shape[0]
    hhi = h.astype(BF16)
    hlo = (h - hhi.astype(F32)).astype(BF16)
    whi = whi_ref[...]
    logits = _dot(hhi, whi) + _dot(hlo, whi) + _dot(hhi, wlo_ref[...]) + b_ref[...]

    lane = lax.broadcasted_iota(jnp.int32, (tm, LANES), 1)
    work = jnp.where(lane < E, logits, NEG_BIG)
    vals, idxs = [], []
    for _ in range(TOP_K):
        m = jnp.max(work, axis=-1, keepdims=True)
        am = jnp.min(jnp.where(work == m, lane, LANES), axis=-1, keepdims=True)
        vals.append(m)
        idxs.append(am)
        work = jnp.where(lane == am, -jnp.inf, work)
    es = [jnp.exp(v - vals[0]) for v in vals]
    denom = es[0] + es[1] + es[2] + es[3]

    sel = (work == -jnp.inf).astype(BF16)
    r = lax.broadcasted_iota(jnp.int32, (tm, tm), 0)
    c = lax.broadcasted_iota(jnp.int32, (tm, tm), 1)
    before = _dot((c < r).astype(BF16), sel) + carry_ref[0:1, :]
    carry_ref[...] = carry_ref[...] + jnp.sum(sel.astype(F32), axis=0, keepdims=True)
    cnt_ref[...] = carry_ref[...]

    idx_o = jnp.zeros((tm, LANES), jnp.int32)
    gate_o = jnp.zeros((tm, LANES), F32)
    rank_o = jnp.zeros((tm, LANES), jnp.int32)
    for k in range(TOP_K):
        rk = jnp.sum(jnp.where(lane == idxs[k], before, 0.0), axis=-1, keepdims=True)
        idx_o = jnp.where(lane == k, idxs[k], idx_o)
        gate_o = jnp.where(lane == k, es[k] / denom, gate_o)
        rank_o = jnp.where(lane == k, rk.astype(jnp.int32), rank_o)
    idx_ref[...] = idx_o
    gate_ref[...] = gate_o
    rank_ref[...] = rank_o


def _router(x1, g, whi, wlo, b, E, tm=256):
    n, d = x1.shape
    tm = min(tm, n)
    full = lambda i: (0, 0)
    tok = pl.BlockSpec((tm, LANES), lambda i: (i, 0))
    return pl.pallas_call(
        functools.partial(_router_kernel, E),
        out_shape=(jax.ShapeDtypeStruct((n, LANES), jnp.int32), jax.ShapeDtypeStruct((n, LANES), F32),
                   jax.ShapeDtypeStruct((n, LANES), jnp.int32), jax.ShapeDtypeStruct((8, LANES), F32)),
        grid=(n // tm,),
        in_specs=[pl.BlockSpec((tm, d), lambda i: (i, 0)), pl.BlockSpec((1, d), full),
                  pl.BlockSpec((d, LANES), full), pl.BlockSpec((d, LANES), full),
                  pl.BlockSpec((1, LANES), full)],
        out_specs=(tok, tok, tok, pl.BlockSpec((8, LANES), full)),
        scratch_shapes=[pltpu.VMEM((8, LANES), F32)],
        compiler_params=_params(40, 1),
        name="router",
    )(x1, g, whi, wlo, b)


def _row_copy(src_hbm, dst_vmem, tok, slot, sem):
    return pltpu.make_async_copy(src_hbm.at[pl.ds(tok, 1), :], dst_vmem.at[pl.ds(slot, 1), :], sem)


def _expert_kernel(tm, be_ref, nv_ref, nused_ref, slots_hbm, x1_hbm, g_ref, wg_ref, wu_ref, wd_ref,
                   bg_ref, bu_ref, bd_ref, y_hbm, idx_smem, xb_ref, acc_ref, sem_idx, sem_in, sem_out):
    i = pl.program_id(0)
    f = pl.program_id(1)
    nf = pl.num_programs(1)

    @pl.when(i < nused_ref[0])
    def _():
        nv = nv_ref[i]

        @pl.when(f == 0)
        def _():
            cp = pltpu.make_async_copy(slots_hbm.at[i], idx_smem, sem_idx)
            cp.start()

            @pl.when(nv < tm)
            def _():
                acc_ref[...] = jnp.zeros_like(acc_ref)

            cp.wait()

            def issue(s, carry):
                _row_copy(x1_hbm, acc_ref, idx_smem[s], s, sem_in).start()
                return carry

            lax.fori_loop(0, nv, issue, 0)

            def drain(s, carry):
                _row_copy(x1_hbm, acc_ref, 0, s, sem_in).wait()
                return carry

            lax.fori_loop(0, nv, drain, 0)
            xb_ref[...] = _rms(acc_ref[...], g_ref[...]).astype(BF16)
            acc_ref[...] = jnp.broadcast_to(bd_ref[0], acc_ref.shape)

        x = xb_ref[...]
        gg = jnp.minimum(_dot(x, wg_ref[0]) + bg_ref[0], SWIGLU_LIMIT)
        uu = jnp.clip(_dot(x, wu_ref[0]) + bu_ref[0], -SWIGLU_LIMIT, SWIGLU_LIMIT)
        act = ((uu + 1.0) * (gg * jax.nn.sigmoid(gg * SWIGLU_ALPHA))).astype(BF16)
        acc_ref[...] += _dot(act, wd_ref[0])

        @pl.when(f == nf - 1)
        def _():
            def issue(s, carry):
                pltpu.make_async_copy(acc_ref.at[pl.ds(s, 1), :],
                                      y_hbm.at[pl.ds(idx_smem[tm + s], 1), :], sem_out).start()
                return carry

            lax.fori_loop(0, nv, issue, 0)

            def drain(s, carry):
                pltpu.make_async_copy(acc_ref.at[pl.ds(s, 1), :],
                                      y_hbm.at[pl.ds(0, 1), :], sem_out).wait()
                return carry

            lax.fori_loop(0, nv, drain, 0)


def _experts(block_expert, block_nvalid, nused, slots, x1, g, wgu, wd, bgu, bd, y_rows, tm, tf):
    n, d = x1.shape
    E, _, f2 = wgu.shape
    F = f2 // 2
    nb = slots.shape[0]
    nf = F // tf

    def fidx(i, f, nu):
        return jnp.where(i < nu[0], f, nf - 1)

    def bidx(i, be, nu):
        return be[jnp.minimum(i, nu[0] - 1)]

    grid_spec = pltpu.PrefetchScalarGridSpec(
        num_scalar_prefetch=3, grid=(nb, nf),
        in_specs=[pl.BlockSpec(memory_space=pl.ANY), pl.BlockSpec(memory_space=pl.ANY),
                  pl.BlockSpec((1, d), lambda i, f, be, nv, nu: (0, 0)),
                  pl.BlockSpec((1, d, tf), lambda i, f, be, nv, nu: (bidx(i, be, nu), 0, fidx(i, f, nu))),
                  pl.BlockSpec((1, d, tf), lambda i, f, be, nv, nu: (bidx(i, be, nu), 0, nf + fidx(i, f, nu))),
                  pl.BlockSpec((1, tf, d), lambda i, f, be, nv, nu: (bidx(i, be, nu), fidx(i, f, nu), 0)),
                  pl.BlockSpec((1, 1, tf), lambda i, f, be, nv, nu: (bidx(i, be, nu), 0, fidx(i, f, nu))),
                  pl.BlockSpec((1, 1, tf), lambda i, f, be, nv, nu: (bidx(i, be, nu), 0, nf + fidx(i, f, nu))),
                  pl.BlockSpec((1, 1, d), lambda i, f, be, nv, nu: (bidx(i, be, nu), 0, 0))],
        out_specs=pl.BlockSpec(memory_space=pl.ANY),
        scratch_shapes=[pltpu.SMEM((2 * tm,), jnp.int32), pltpu.VMEM((tm, d), BF16), pltpu.VMEM((tm, d), F32),
                        pltpu.SemaphoreType.DMA, pltpu.SemaphoreType.DMA, pltpu.SemaphoreType.DMA])
    return pl.pallas_call(
        functools.partial(_expert_kernel, tm),
        out_shape=jax.ShapeDtypeStruct((y_rows, d), F32),
        grid_spec=grid_spec,
        compiler_params=_params(56, 2),
        name="experts",
    )(block_expert, block_nvalid, nused, slots, x1, g, wgu, wgu, wd, bgu, bgu, bd)


def _combine_kernel(x_ref, gate_ref, y0_ref, y1_ref, y2_ref, y3_ref, g_ref, o_ref):
    acc = x_ref[...]
    for k, y_ref in enumerate((y0_ref, y1_ref, y2_ref, y3_ref)):
        acc = acc + gate_ref[:, k:k + 1] * y_ref[...]
    o_ref[...] = _rms(acc, g_ref[...])


def _combine(x1, gate, y, g, row0, rows, n_tok, tm=128):
    d = x1.shape[1]
    tm = min(tm, rows)
    b0 = row0 // tm
    nbt = n_tok // tm
    ys = [pl.BlockSpec((tm, d), functools.partial(lambda i, k: (k * nbt + b0 + i, 0), k=k)) for k in range(TOP_K)]
    return pl.pallas_call(
        _combine_kernel,
        out_shape=jax.ShapeDtypeStruct((rows, d), F32),
        grid=(rows // tm,),
        in_specs=[pl.BlockSpec((tm, d), lambda i: (b0 + i, 0)),
                  pl.BlockSpec((tm, LANES), lambda i: (b0 + i, 0))] + ys +
                 [pl.BlockSpec((1, d), lambda i: (0, 0))],
        out_specs=pl.BlockSpec((tm, d), lambda i: (i, 0)),
        compiler_params=_params(48, 1),
        name="combine",
    )(x1, gate, y, y, y, y, g)


def _gate_weight_tables(w_g, b_g, H):
    d = w_g.shape[0]
    wi = jnp.stack([w_g[:, 0:H], w_g[:, 2 * H:3 * H]], axis=-1)
    wf = jnp.stack([w_g[:, H:2 * H], w_g[:, 3 * H:4 * H]], axis=-1)
    bi = jnp.stack([b_g[0:H], b_g[2 * H:3 * H]], axis=-1)
    bf = jnp.stack([b_g[H:2 * H], b_g[3 * H:4 * H]], axis=-1)

    def col(w):
        w6 = jnp.repeat(w, 3, axis=-1)
        w16 = jnp.pad(w6, [(0, 0)] * (w6.ndim - 1) + [(0, 10)]).reshape(w.shape[:-2] + (H * 16,))
        return jnp.pad(w16, [(0, 0)] * (w16.ndim - 1) + [(0, LANES - H * 16)])

    def rowt(w):
        return jnp.pad(w, ((0, 16 - 2 * H), (0, 0)))

    wit = rowt(wi.reshape(d, 2 * H).T)
    wft = rowt(wf.reshape(d, 2 * H).T)
    return (col(wi).astype(BF16), col(wf).astype(BF16), col(bi)[None].astype(F32), col(bf)[None].astype(F32),
            wit.astype(BF16), wft.astype(BF16), rowt(bi.reshape(2 * H, 1)).astype(F32),
            rowt(bf.reshape(2 * H, 1)).astype(F32))


def _mlstm_schedule(seq_lens, L):
    fwd, bwd, reset, base = [], [], [], 0
    for t in seq_lens:
        nc = t // L
        for c in range(nc):
            fwd.append(base + c)
            bwd.append(base + nc - 1 - c)
            reset.append(1 if c == 0 else 0)
        base += nc
    as_i32 = lambda v: jnp.asarray(v, dtype=jnp.int32)
    return as_i32(fwd), as_i32(bwd), as_i32(reset)


def kernel(x_prompt, x_sample, g_mix, w_in, b_gates, g_mlstm_out, g_na_out, rpb, w_out, g_ffn, w_router,
           b_router, w_gate_up, b_gate_up, w_down, b_down, g_final):
    assert g_mix.shape[0] == 1, "single layer"
    bp, tp, d = x_prompt.shape
    bs, ts, _ = x_sample.shape
    np_, ns = bp * tp, bs * ts
    n = np_ + ns
    H = b_gates.shape[1] // 4
    na_heads = rpb.shape[1]
    wm = g_mlstm_out.shape[1]
    wa = g_na_out.shape[1]
    dv = wm // H
    dk = dv // 2
    da = wa // na_heads
    E = w_router.shape[2]
    F = w_down.shape[2]
    L = min(MLSTM_CHUNK, tp, ts)
    assert H <= 8 and dk == LANES and da == LANES and E <= LANES and np_ % ts == 0

    xp = x_prompt.reshape(np_, d)
    xs = x_sample.reshape(ns, d)

    w = w_in[0]
    o0 = 2 * H * dk + 2 * wm
    w_main = jnp.concatenate([w[:, :o0], w[:, o0 + 4 * H:]], axis=1).astype(BF16)
    gate_tabs = _gate_weight_tables(w[:, o0:o0 + 4 * H], b_gates[0], H)
    q_off, k_off, v_off, om_off = 0, H * dk, 2 * H * dk, 2 * H * dk + wm
    qa_off, ka_off, va_off = o0, o0 + wa, o0 + 2 * wa

    h = _norm_in(xp, xs, g_mix)
    proj = _matmul(h, w_main)
    col, row = _gates(h, *gate_tabs, L)
    sched = _mlstm_schedule([tp] * bp + [ts] * bs, L)
    hf, hb = _mlstm(proj, col, row, sched, H, L, dk, dv, q_off, k_off, v_off)
    bias = _na_bias(rpb[0])
    hap = _na(proj, bias, g_na_out, bp, 0, tp, na_heads, da, qa_off, ka_off, va_off)
    has_ = _na(proj, bias, g_na_out, bs, np_, ts, na_heads, da, qa_off, ka_off, va_off)
    x1 = _out_proj(hf, hb, proj, om_off, hap, has_, g_mlstm_out, w_out[0].astype(BF16), xp, xs, H, dv)

    wr = jnp.pad(w_router[0], ((0, 0), (0, LANES - E)))
    wr_hi = wr.astype(BF16)
    wr_lo = (wr - wr_hi.astype(F32)).astype(BF16)
    br = jnp.pad(b_router, ((0, 0), (0, LANES - E)))
    idx, gate, rank, cnt = _router(x1, g_ffn, wr_hi, wr_lo, br, E)
    idx, rank = idx[:, :TOP_K], rank[:, :TOP_K]

    tm = 512 if (n * TOP_K) % 512 == 0 and n * TOP_K >= 512 * E else 256
    tf = min(512, F)
    nb = n * TOP_K // tm + E
    counts = cnt[0, :E].astype(jnp.int32)
    nblk = (counts + tm - 1) // tm
    pend = jnp.cumsum(nblk) * tm
    pstart = pend - nblk * tm
    dest = (pstart[idx] + rank).reshape(-1)
    tok = jnp.repeat(jnp.arange(n, dtype=jnp.int32), TOP_K)
    kk = jnp.tile(jnp.arange(TOP_K, dtype=jnp.int32), n)
    slot_tok = jnp.zeros((nb * tm,), jnp.int32).at[dest].set(tok)
    slot_dst = jnp.zeros((nb * tm,), jnp.int32).at[dest].set(kk * n + tok)
    slots = jnp.concatenate([slot_tok.reshape(nb, tm), slot_dst.reshape(nb, tm)], axis=1)
    blk0 = jnp.arange(nb, dtype=jnp.int32) * tm
    block_expert = jnp.minimum(jnp.searchsorted(pend, blk0, side='right'), E - 1).astype(jnp.int32)
    block_nvalid = jnp.clip(counts[block_expert] - (blk0 - pstart[block_expert]), 0, tm).astype(jnp.int32)
    nused = (pend[-1:] // tm).astype(jnp.int32)

    y = _experts(block_expert, block_nvalid, nused, slots, x1, g_ffn, w_gate_up[0].astype(BF16),
                 w_down[0].astype(BF16), b_gate_up[0][:, None, :], b_down[0][:, None, :], n * TOP_K, tm, tf)

    gf = g_final[None, :]
    y_prompt = _combine(x1, gate, y, gf, 0, np_, n).reshape(bp, tp, d)
    y_sample = _combine(x1, gate, y, gf, np_, ns, n).reshape(bs, ts, d)
    return (y_prompt, y_sample)
```

```python
import functools

import jax
import jax.numpy as jnp
from jax import lax
from jax.experimental import pallas as pl
from jax.experimental.pallas import tpu as pltpu

GRID_W = 64
GATE_SOFTCAP = 15.0
NA_WIN_ROWS = 8
NA_WIN_COLS = 16
TOP_K = 4
SWIGLU_LIMIT = 7.0
SWIGLU_ALPHA = 1.702
RMS_EPS = 1e-6

LANES = 128
SUBLANES = 8
MLSTM_CHUNK = 256
NA_ROWS_PER_STEP = 8
NEG_BIG = -1e30
MIB = 1024 * 1024

F32 = jnp.float32
BF16 = jnp.bfloat16


def _params(vmem_mib, n_axes):
    return pltpu.CompilerParams(
        dimension_semantics=("arbitrary",) * n_axes, vmem_limit_bytes=vmem_mib * MIB)


def _dot(a, b):
    return jnp.dot(a, b, preferred_element_type=F32)


def _dot_nt(a, b):
    return lax.dot_general(a, b, (((1,), (1,)), ((), ())), preferred_element_type=F32)


def _dot_tn(a, b):
    return lax.dot_general(a, b, (((0,), (0,)), ((), ())), preferred_element_type=F32)


def _split3(x):
    hi = x.astype(BF16)
    r1 = x - hi.astype(F32)
    mid = r1.astype(BF16)
    lo = (r1 - mid.astype(F32)).astype(BF16)
    return hi, mid, lo


def _rms(x, g):
    return x * lax.rsqrt(jnp.mean(x * x, axis=-1, keepdims=True) + RMS_EPS) * g


def _norm_in_kernel(nbp, xp_ref, xs_ref, g_ref, o_ref):
    i = pl.program_id(0)
    x = jnp.where(i < nbp, xp_ref[...], xs_ref[...])
    o_ref[...] = _rms(x, g_ref[...]).astype(o_ref.dtype)


def _norm_in(xp, xs, g, tm=256):
    np_, d = xp.shape
    ns = xs.shape[0]
    nbp, nbs = np_ // tm, ns // tm
    return pl.pallas_call(
        functools.partial(_norm_in_kernel, nbp),
        out_shape=jax.ShapeDtypeStruct((np_ + ns, d), BF16),
        grid=(nbp + nbs,),
        in_specs=[
            pl.BlockSpec((tm, d), lambda i: (jnp.minimum(i, nbp - 1), 0)),
            pl.BlockSpec((tm, d), lambda i: (jnp.maximum(i - nbp, 0), 0)),
            pl.BlockSpec((1, d), lambda i: (0, 0)),
        ],
        out_specs=pl.BlockSpec((tm, d), lambda i: (i, 0)),
        compiler_params=_params(48, 1),
        name="norm_in",
    )(xp, xs, g)


def _matmul_kernel(a_ref, b_ref, o_ref):
    o_ref[...] = _dot(a_ref[...], b_ref[...]).astype(o_ref.dtype)


def _matmul(a, b, tm=1024, tn=1024):
    m, k = a.shape
    n = b.shape[1]
    tm, tn = min(tm, m), min(tn, n)
    return pl.pallas_call(
        _matmul_kernel,
        out_shape=jax.ShapeDtypeStruct((m, n), BF16),
        grid=(m // tm, n // tn),
        in_specs=[pl.BlockSpec((tm, k), lambda i, j: (i, 0)),
                  pl.BlockSpec((k, tn), lambda i, j: (0, j))],
        out_specs=pl.BlockSpec((tm, tn), lambda i, j: (i, j)),
        compiler_params=_params(48, 2),
        name="in_proj",
    )(a, b)


def _softcap(x):
    return GATE_SOFTCAP * jnp.tanh(x / GATE_SOFTCAP)


def _log_sigmoid(x):
    return jnp.minimum(x, 0.0) - jnp.log1p(jnp.exp(-jnp.abs(x)))


def _gates_kernel(L, h_ref, wi_ref, wf_ref, bi_ref, bf_ref, wit_ref, wft_ref, bit_ref, bft_ref,
                  col_ref, row_ref):
    h = h_ref[...]
    t = h.shape[0]
    gi = _softcap(_dot(h, wi_ref[...]) + bi_ref[...])
    lf = _log_sigmoid(_softcap(_dot(h, wf_ref[...]) + bf_ref[...]))
    git = _softcap(_dot_nt(wit_ref[...], h) + bit_ref[...])
    lft = _log_sigmoid(_softcap(_dot_nt(wft_ref[...], h) + bft_ref[...]))

    r = lax.broadcasted_iota(jnp.int32, (L, L), 0)
    c = lax.broadcasted_iota(jnp.int32, (L, L), 1)
    tri = (c <= r).astype(BF16)
    trit = (r <= c).astype(BF16)

    lane = lax.broadcasted_iota(jnp.int32, (L, LANES), 1)
    q = lane % 16
    is_bwd = q >= 3
    kind = jnp.where(is_bwd, q - 3, q)
    rowi = lax.broadcasted_iota(jnp.int32, (16, L), 0)
    row_bwd = (rowi % 2) == 1

    for ci in range(t // L):
        sl = slice(ci * L, (ci + 1) * L)
        lfc, gic = lf[sl], gi[sl]
        hi, mid, lo = _split3(lfc)
        b = _dot(tri, hi) + _dot(tri, mid) + _dot(tri, lo)
        a = b - lfc
        tot = b[L - 1:L, :]
        alpha = jnp.where(is_bwd, -a, b)
        gamma = jnp.where(is_bwd, tot - a, b)
        omega = jnp.where(is_bwd, a + gic, tot - b + gic)
        col_ref[sl, :] = jnp.where(kind == 0, alpha, jnp.where(kind == 1, gamma, omega))

        lfct, gict = lft[:, sl], git[:, sl]
        hi, mid, lo = _split3(lfct)
        bt = _dot(hi, trit) + _dot(mid, trit) + _dot(lo, trit)
        at = bt - lfct
        row_ref[:, sl] = jnp.where(row_bwd, at + gict, gict - bt)


def _gates(h, wi, wf, bi, bf, wit, wft, bit, bft, L, tm=1024):
    n, d = h.shape
    tm = min(tm, n)
    full = lambda i: (0, 0)
    return pl.pallas_call(
        functools.partial(_gates_kernel, L),
        out_shape=(jax.ShapeDtypeStruct((n, LANES), F32), jax.ShapeDtypeStruct((16, n), F32)),
        grid=(n // tm,),
        in_specs=[pl.BlockSpec((tm, d), lambda i: (i, 0)),
                  pl.BlockSpec((d, LANES), full), pl.BlockSpec((d, LANES), full),
                  pl.BlockSpec((1, LANES), full), pl.BlockSpec((1, LANES), full),
                  pl.BlockSpec((16, d), full), pl.BlockSpec((16, d), full),
                  pl.BlockSpec((16, 1), full), pl.BlockSpec((16, 1), full)],
        out_specs=(pl.BlockSpec((tm, LANES), lambda i: (i, 0)),
                   pl.BlockSpec((16, tm), lambda i: (0, i))),
        compiler_params=_params(40, 1),
        name="mlstm_gates",
    )(h, wi, wf, bi, bf, wit, wft, bit, bft)


def _mlstm_kernel(H, L, dk, dv, fwd_ref, bwd_ref, reset_ref,
                  qf_ref, kf_ref, vf_ref, colf_ref, rowf_ref,
                  qb_ref, kb_ref, vb_ref, colb_ref, rowb_ref,
                  hf_ref, hb_ref, ct_ref, nr_ref):
    s = pl.program_id(0)

    @pl.when(reset_ref[s] == 1)
    def _():
        ct_ref[...] = jnp.zeros_like(ct_ref)
        nr_ref[...] = jnp.zeros_like(nr_ref)

    r = lax.broadcasted_iota(jnp.int32, (L, L), 0)
    c = lax.broadcasted_iota(jnp.int32, (L, L), 1)
    masks = (c <= r, c >= r)
    ones = jnp.ones((L, LANES), BF16)
    scale = dk ** -0.5
    dirs = ((qf_ref, kf_ref, vf_ref, colf_ref, rowf_ref, hf_ref),
            (qb_ref, kb_ref, vb_ref, colb_ref, rowb_ref, hb_ref))

    for hd in range(H):
        for d, (q_ref, k_ref, v_ref, col_ref, row_ref, o_ref) in enumerate(dirs):
            ch = hd * 2 + d
            q = q_ref[:, hd * dk:(hd + 1) * dk]
            k = k_ref[:, hd * dk:(hd + 1) * dk]
            v = v_ref[:, hd * dv:(hd + 1) * dv]
            lane0 = hd * 16 + d * 3
            alpha = col_ref[:, lane0:lane0 + 1]
            gamma = col_ref[:, lane0 + 1:lane0 + 2]
            omega = col_ref[:, lane0 + 2:lane0 + 3]
            beta = row_ref[ch:ch + 1, :]
            g_end = gamma[L - 1:L, :] if d == 0 else gamma[0:1, :]

            logd = jnp.where(masks[d], alpha + beta, -jnp.inf)
            p = (_dot_nt(q, k) * scale) * jnp.exp(logd)
            pb = p.astype(BF16)
            ct = ct_ref[ch]
            nr = nr_ref[ch]
            eg = jnp.exp(gamma) * scale
            num = _dot(pb, v) + eg * _dot(q, ct.astype(BF16))
            den = _dot(pb, ones) + eg * _dot(q, nr.astype(BF16))
            inv = 1.0 / jnp.maximum(jnp.abs(den), 1.0)
            o_ref[:, hd * dv:(hd + 1) * dv] = (
                num * jnp.concatenate([inv] * (dv // LANES), axis=1)).astype(o_ref.dtype)

            kw = (k.astype(F32) * jnp.exp(omega)).astype(BF16)
            decay = jnp.exp(g_end)
            ct_ref[ch] = decay * ct + _dot_tn(kw, v)
            nr_ref[ch] = decay * nr + _dot_tn(kw, ones)


def _mlstm(proj, col, row, sched, H, L, dk, dv, q_off, k_off, v_off):
    n = proj.shape[0]
    fwd, bwd, reset = sched
    steps = fwd.shape[0]
    qw, vw = H * dk, H * dv
    qb, kb, vb = q_off // qw, k_off // qw, v_off // vw
    fi = lambda s, f, b, r: f[s]
    bi = lambda s, f, b, r: b[s]

    def specs(pick):
        return [pl.BlockSpec((L, qw), lambda s, f, b, r: (pick(s, f, b, r), qb)),
                pl.BlockSpec((L, qw), lambda s, f, b, r: (pick(s, f, b, r), kb)),
                pl.BlockSpec((L, vw), lambda s, f, b, r: (pick(s, f, b, r), vb)),
                pl.BlockSpec((L, LANES), lambda s, f, b, r: (pick(s, f, b, r), 0)),
                pl.BlockSpec((16, L), lambda s, f, b, r: (0, pick(s, f, b, r)))]

    grid_spec = pltpu.PrefetchScalarGridSpec(
        num_scalar_prefetch=3, grid=(steps,),
        in_specs=specs(fi) + specs(bi),
        out_specs=(pl.BlockSpec((L, vw), lambda s, f, b, r: (f[s], 0)),
                   pl.BlockSpec((L, vw), lambda s, f, b, r: (b[s], 0))),
        scratch_shapes=[pltpu.VMEM((2 * H, dk, dv), F32), pltpu.VMEM((2 * H, dk, LANES), F32)])
    return pl.pallas_call(
        functools.partial(_mlstm_kernel, H, L, dk, dv),
        out_shape=(jax.ShapeDtypeStruct((n, vw), BF16), jax.ShapeDtypeStruct((n, vw), BF16)),
        grid_spec=grid_spec,
        compiler_params=_params(40, 1),
        name="mlstm",
    )(fwd, bwd, reset, proj, proj, proj, col, row, proj, proj, proj, col, row)


def _na_kernel(rows, d, q_ref, k_ref, v_ref, bias_ref, g_ref, o_ref):
    scale = d ** -0.5
    win = NA_WIN_ROWS * GRID_W
    nq = NA_ROWS_PER_STEP * GRID_W
    g = g_ref[...]
    ones = jnp.ones((win, LANES), BF16)

    def block(rb, carry):
        starts, logits = [], []
        for u in range(NA_ROWS_PER_STEP):
            r = rb * NA_ROWS_PER_STEP + u
            rs = jnp.clip(r - NA_WIN_ROWS // 2, 0, rows - NA_WIN_ROWS)
            ks = pl.multiple_of(rs * GRID_W, GRID_W)
            q = q_ref[pl.ds(pl.multiple_of(r * GRID_W, GRID_W), GRID_W), :]
            logits.append(_dot_nt(q, k_ref[pl.ds(ks, win), :]) * scale + bias_ref[r - rs])
            starts.append(ks)
        s = jnp.concatenate(logits, axis=0)
        pb = jnp.exp(s - jnp.max(s, axis=-1, keepdims=True)).astype(BF16)
        l = _dot(pb, ones)
        o = jnp.concatenate(
            [_dot(pb[u * GRID_W:(u + 1) * GRID_W], v_ref[pl.ds(starts[u], win), :])
             for u in range(NA_ROWS_PER_STEP)], axis=0) / l
        o_ref[pl.ds(pl.multiple_of(rb * nq, nq), nq), :] = _rms(o, g).astype(o_ref.dtype)
        return carry

    lax.fori_loop(0, rows // NA_ROWS_PER_STEP, block, 0)


def _na(proj, bias, g, nseq, row0, T, heads, d, q_off, k_off, v_off):
    rows = T // GRID_W
    sb = row0 // T
    qb, kb, vb = q_off // d, k_off // d, v_off // d
    return pl.pallas_call(
        functools.partial(_na_kernel, rows, d),
        out_shape=jax.ShapeDtypeStruct((nseq * T, heads * d), BF16),
        grid=(nseq, heads),
        in_specs=[pl.BlockSpec((T, d), lambda b, h: (sb + b, qb + h)),
                  pl.BlockSpec((T, d), lambda b, h: (sb + b, kb + h)),
                  pl.BlockSpec((T, d), lambda b, h: (sb + b, vb + h)),
                  pl.BlockSpec((None, NA_WIN_ROWS, GRID_W, NA_WIN_ROWS * GRID_W), lambda b, h: (h, 0, 0, 0)),
                  pl.BlockSpec((1, d), lambda b, h: (0, h))],
        out_specs=pl.BlockSpec((T, d), lambda b, h: (b, h)),
        compiler_params=_params(48, 2),
        name="natten",
    )(proj, proj, proj, bias, g)


def _na_bias(rpb):
    heads = rpb.shape[0]
    ncol = 2 * NA_WIN_COLS - 1
    cq = jnp.arange(GRID_W)
    cs = jnp.clip(cq - NA_WIN_COLS // 2, 0, GRID_W - NA_WIN_COLS)
    j = jnp.arange(GRID_W)
    inside = (j[None, :] >= cs[:, None]) & (j[None, :] < cs[:, None] + NA_WIN_COLS)
    coff = j[None, :] - cq[:, None] + (NA_WIN_COLS - 1)
    pick = ((coff[None] == jnp.arange(ncol)[:, None, None]) & inside[None]).astype(F32)
    rsel = jnp.stack([rpb.astype(F32)[:, NA_WIN_ROWS - 1 - var:2 * NA_WIN_ROWS - 1 - var, :]
                      for var in range(NA_WIN_ROWS)], axis=1)
    tab = jnp.einsum('hvic,cqj->hvqij', rsel, pick, precision=lax.Precision.HIGHEST)
    tab = jnp.where(inside[None, None, :, None, :], tab, NEG_BIG)
    return tab.reshape(heads, NA_WIN_ROWS, GRID_W, NA_WIN_ROWS * GRID_W)


def _out_proj_kernel(nbp, H, dv, hf_ref, hb_ref, om_ref, hap_ref, has_ref, g_ref, w_ref,
                     xp_ref, xs_ref, o_ref, lhs_ref):
    i = pl.program_id(0)
    j = pl.program_id(1)
    wm = H * dv

    @pl.when(j == 0)
    def _():
        for hd in range(H):
            sl = slice(hd * dv, (hd + 1) * dv)
            hs = hf_ref[:, sl].astype(F32) + hb_ref[:, sl].astype(F32)
            y = _rms(hs, g_ref[:, sl]) * jax.nn.sigmoid(om_ref[:, sl].astype(F32))
            lhs_ref[:, sl] = y.astype(BF16)
        lhs_ref[:, wm:] = jnp.where(i < nbp, hap_ref[...], has_ref[...])

    x = jnp.where(i < nbp, xp_ref[...], xs_ref[...])
    o_ref[...] = x + _dot(lhs_ref[...], w_ref[...])


def _out_proj(hf, hb, proj, om_off, hap, has_, g_ml, w_out, xp, xs, H, dv, tm=512, tn=512):
    n, wm = hf.shape
    wa = hap.shape[1]
    d = w_out.shape[1]
    tm, tn = min(tm, xp.shape[0], xs.shape[0]), min(tn, d)
    nbp = xp.shape[0] // tm
    omb = om_off // wm
    pidx = lambda i, j: (jnp.minimum(i, nbp - 1), 0)
    sidx = lambda i, j: (jnp.maximum(i - nbp, 0), 0)
    return pl.pallas_call(
        functools.partial(_out_proj_kernel, nbp, H, dv),
        out_shape=jax.ShapeDtypeStruct((n, d), F32),
        grid=(n // tm, d // tn),
        in_specs=[pl.BlockSpec((tm, wm), lambda i, j: (i, 0)),
                  pl.BlockSpec((tm, wm), lambda i, j: (i, 0)),
                  pl.BlockSpec((tm, wm), lambda i, j: (i, omb)),
                  pl.BlockSpec((tm, wa), pidx),
                  pl.BlockSpec((tm, wa), sidx),
                  pl.BlockSpec((1, wm), lambda i, j: (0, 0)),
                  pl.BlockSpec((wm + wa, tn), lambda i, j: (0, j)),
                  pl.BlockSpec((tm, tn), lambda i, j: (jnp.minimum(i, nbp - 1), j)),
                  pl.BlockSpec((tm, tn), lambda i, j: (jnp.maximum(i - nbp, 0), j))],
        out_specs=pl.BlockSpec((tm, tn), lambda i, j: (i, j)),
        scratch_shapes=[pltpu.VMEM((tm, wm + wa), BF16)],
        compiler_params=_params(48, 2),
        name="out_proj",
    )(hf, hb, proj, hap, has_, g_ml, w_out, xp, xs)


def _pack_bf16_pair(lo, hi):
    lo_bits = lax.bitcast_convert_type(lo.astype(BF16).astype(F32), jnp.uint32)
    hi_bits = lax.bitcast_convert_type(hi.astype(BF16).astype(F32), jnp.uint32)
    return hi_bits | (lo_bits >> 16)


def _unpack_bf16_pair(w):
    lo = lax.bitcast_convert_type(w << 16, F32)
    hi = lax.bitcast_convert_type(w & jnp.uint32(0xFFFF0000), F32)
    return lo, hi


def _router_kernel(E, x_ref, g_ref, whi_ref, wlo_ref, b_ref, idx_ref, gate_ref, rank_ref, cnt_ref, hpk_ref,
                   carry_ref):
    i = pl.program_id(0)

    @pl.when(i == 0)
    def _():
        carry_ref[...] = jnp.zeros_like(carry_ref)

    h = _rms(x_ref[...], g_ref[...])
    tm, d = h.shape
    hpk_ref[...] = _pack_bf16_pair(h[:, :d // 2], h[:, d // 2:])
    hhi = h.astype(BF16)
    hlo = (h - hhi.astype(F32)).astype(BF16)
    whi = whi_ref[...]
    logits = _dot(hhi, whi) + _dot(hlo, whi) + _dot(hhi, wlo_ref[...]) + b_ref[...]

    lane = lax.broadcasted_iota(jnp.int32, (tm, LANES), 1)
    work = jnp.where(lane < E, logits, NEG_BIG)
    vals, idxs = [], []
    for _ in range(TOP_K):
        m = jnp.max(work, axis=-1, keepdims=True)
        am = jnp.min(jnp.where(work == m, lane, LANES), axis=-1, keepdims=True)
        vals.append(m)
        idxs.append(am)
        work = jnp.where(lane == am, -jnp.inf, work)
    es = [jnp.exp(v - vals[0]) for v in vals]
    denom = es[0] + es[1] + es[2] + es[3]

    sel = (work == -jnp.inf).astype(BF16)
    r = lax.broadcasted_iota(jnp.int32, (tm, tm), 0)
    c = lax.broadcasted_iota(jnp.int32, (tm, tm), 1)
    before = _dot((c < r).astype(BF16), sel) + carry_ref[0:1, :]
    carry_ref[...] = carry_ref[...] + jnp.sum(sel.astype(F32), axis=0, keepdims=True)
    cnt_ref[...] = carry_ref[...]

    idx_o = jnp.zeros((tm, LANES), jnp.int32)
    gate_o = jnp.zeros((tm, LANES), F32)
    rank_o = jnp.zeros((tm, LANES), jnp.int32)
    for k in range(TOP_K):
        rk = jnp.sum(jnp.where(lane == idxs[k], before, 0.0), axis=-1, keepdims=True)
        idx_o = jnp.where(lane == k, idxs[k], idx_o)
        gate_o = jnp.where(lane == k, es[k] / denom, gate_o)
        rank_o = jnp.where(lane == k, rk.astype(jnp.int32), rank_o)
    idx_ref[...] = idx_o
    gate_ref[...] = gate_o
    rank_ref[...] = rank_o


def _router(x1, g, whi, wlo, b, E, tm=256):
    n, d = x1.shape
    tm = min(tm, n)
    full = lambda i: (0, 0)
    tok = pl.BlockSpec((tm, LANES), lambda i: (i, 0))
    return pl.pallas_call(
        functools.partial(_router_kernel, E),
        out_shape=(jax.ShapeDtypeStruct((n, LANES), jnp.int32), jax.ShapeDtypeStruct((n, LANES), F32),
                   jax.ShapeDtypeStruct((n, LANES), jnp.int32), jax.ShapeDtypeStruct((8, LANES), F32),
                   jax.ShapeDtypeStruct((n, d // 2), jnp.uint32)),
        grid=(n // tm,),
        in_specs=[pl.BlockSpec((tm, d), lambda i: (i, 0)), pl.BlockSpec((1, d), full),
                  pl.BlockSpec((d, LANES), full), pl.BlockSpec((d, LANES), full),
                  pl.BlockSpec((1, LANES), full)],
        out_specs=(tok, tok, tok, pl.BlockSpec((8, LANES), full), pl.BlockSpec((tm, d // 2), lambda i: (i, 0))),
        scratch_shapes=[pltpu.VMEM((8, LANES), F32)],
        compiler_params=_params(40, 1),
        name="router",
    )(x1, g, whi, wlo, b)


def _expert_kernel(tm, n_tok, be_ref, nv_ref, nused_ref, slots_hbm, hpk_hbm, wg_ref, wu_ref, wd_ref,
                   bg_ref, bu_ref, bd_ref, y_hbm, idx_smem, gbuf_ref, xb_ref, acc_ref, ybuf_ref,
                   sem_idx, sem_in, sem_out):
    i = pl.program_id(0)
    f = pl.program_id(1)
    nf = pl.num_programs(1)
    nused = nused_ref[0]
    dh = gbuf_ref.shape[2]
    ngrp = tm // SUBLANES


    def idx_base(blk):
        return pl.multiple_of((blk % 2) * tm, tm)

    def idx_fetch(blk):
        return pltpu.make_async_copy(slots_hbm.at[blk], idx_smem.at[pl.ds(idx_base(blk), tm)], sem_idx)

    def gather_start(blk):
        base = idx_base(blk)

        def group(gi, carry):
            for u in range(SUBLANES):
                tok = lax.shift_right_logical(idx_smem[base + gi * SUBLANES + u], 2)
                pltpu.make_async_copy(hpk_hbm.at[pl.ds(tok, 1), :], gbuf_ref.at[gi, pl.ds(u, 1), :], sem_in).start()
            return carry

        lax.fori_loop(0, ngrp, group, 0)

    def gather_wait():
        def group(gi, carry):
            pltpu.make_async_copy(hpk_hbm.at[pl.ds(0, SUBLANES), :], gbuf_ref.at[0], sem_in).wait()
            return carry

        lax.fori_loop(0, ngrp, group, 0)

    def out_row(gi, u, dst):
        return pltpu.make_async_copy(ybuf_ref.at[gi, pl.ds(u, 1), :], y_hbm.at[pl.ds(dst, 1), :], sem_out)

    def out_group():
        return pltpu.make_async_copy(ybuf_ref.at[0], y_hbm.at[pl.ds(0, SUBLANES), :], sem_out)

    def scatter_start(blk, nv):
        base = idx_base(blk)

        def dst_row(s):
            pair = idx_smem[base + s]
            return (pair & (TOP_K - 1)) * n_tok + lax.shift_right_logical(pair, 2)

        def group(gi, carry):
            for u in range(SUBLANES):
                out_row(gi, u, dst_row(gi * SUBLANES + u)).start()
            return carry

        def single(s, carry):
            out_row(lax.shift_right_logical(s, 3), s & (SUBLANES - 1), dst_row(s)).start()
            return carry

        n8 = lax.shift_right_logical(nv, 3)
        lax.fori_loop(0, n8, group, 0)
        lax.fori_loop(n8 * SUBLANES, nv, single, 0)

    def scatter_wait(nv):
        def group(gi, carry):
            out_group().wait()
            return carry

        def single(s, carry):
            out_row(0, 0, 0).wait()
            return carry

        n8 = lax.shift_right_logical(nv, 3)
        lax.fori_loop(0, n8, group, 0)
        lax.fori_loop(n8 * SUBLANES, nv, single, 0)

    @pl.when(i < nused)
    def _():
        nv = nv_ref[i]

        @pl.when(f == 0)
        def _():
            @pl.when(i == 0)
            def _():
                cp = idx_fetch(0)
                cp.start()
                cp.wait()
                gather_start(0)

            gather_wait()
            lo, hi = _unpack_bf16_pair(gbuf_ref[...].reshape(tm, dh))
            xb_ref[:, :dh] = lo.astype(BF16)
            xb_ref[:, dh:] = hi.astype(BF16)
            acc_ref[...] = jnp.broadcast_to(bd_ref[0], acc_ref.shape)

            @pl.when(i + 1 < nused)
            def _():
                idx_fetch(i + 1).start()

        @pl.when((f == 1) & (i + 1 < nused))
        def _():
            idx_fetch(i + 1).wait()
            gather_start(i + 1)

        def ffn(rows):
            x = xb_ref[:rows, :]
            gg = jnp.minimum(_dot(x, wg_ref[0]) + bg_ref[0], SWIGLU_LIMIT)
            uu = jnp.clip(_dot(x, wu_ref[0]) + bu_ref[0], -SWIGLU_LIMIT, SWIGLU_LIMIT)
            act = ((uu + 1.0) * (gg * jax.nn.sigmoid(gg * SWIGLU_ALPHA))).astype(BF16)
            acc_ref[:rows, :] += _dot(act, wd_ref[0])

        @pl.when(nv > tm // 2)
        def _():
            ffn(tm)

        @pl.when(nv <= tm // 2)
        def _():
            ffn(tm // 2)

        @pl.when(f == nf - 1)
        def _():
            @pl.when(i > 0)
            def _():
                scatter_wait(nv_ref[jnp.maximum(i - 1, 0)])

            y = acc_ref[...]
            ybuf_ref[...] = _pack_bf16_pair(y[:, :dh], y[:, dh:]).reshape(ngrp, SUBLANES, dh)
            scatter_start(i, nv)

            @pl.when(i == nused - 1)
            def _():
                scatter_wait(nv)


def _experts(block_expert, block_nvalid, nused, slots, hpk, wgu, wd, bgu, bd, tm, tf):
    n, dh = hpk.shape
    d = 2 * dh
    E, _, f2 = wgu.shape
    F = f2 // 2
    nb = slots.shape[0]
    nf = F // tf
    assert nf >= 2

    def fidx(i, f, nu):
        return jnp.where(i < nu[0], f, nf - 1)

    def bidx(i, be, nu):
        return be[jnp.minimum(i, nu[0] - 1)]

    grid_spec = pltpu.PrefetchScalarGridSpec(
        num_scalar_prefetch=3, grid=(nb, nf),
        in_specs=[pl.BlockSpec(memory_space=pl.ANY), pl.BlockSpec(memory_space=pl.ANY),
                  pl.BlockSpec((1, d, tf), lambda i, f, be, nv, nu: (bidx(i, be, nu), 0, fidx(i, f, nu))),
                  pl.BlockSpec((1, d, tf), lambda i, f, be, nv, nu: (bidx(i, be, nu), 0, nf + fidx(i, f, nu))),
                  pl.BlockSpec((1, tf, d), lambda i, f, be, nv, nu: (bidx(i, be, nu), fidx(i, f, nu), 0)),
                  pl.BlockSpec((1, 1, tf), lambda i, f, be, nv, nu: (bidx(i, be, nu), 0, fidx(i, f, nu))),
                  pl.BlockSpec((1, 1, tf), lambda i, f, be, nv, nu: (bidx(i, be, nu), 0, nf + fidx(i, f, nu))),
                  pl.BlockSpec((1, 1, d), lambda i, f, be, nv, nu: (bidx(i, be, nu), 0, 0))],
        out_specs=pl.BlockSpec(memory_space=pl.ANY),
        scratch_shapes=[pltpu.SMEM((2 * tm,), jnp.int32), pltpu.VMEM((tm // SUBLANES, SUBLANES, dh), jnp.uint32),
                        pltpu.VMEM((tm, d), BF16), pltpu.VMEM((tm, d), F32), pltpu.VMEM((tm // SUBLANES, SUBLANES, dh), jnp.uint32),
                        pltpu.SemaphoreType.DMA, pltpu.SemaphoreType.DMA, pltpu.SemaphoreType.DMA])
    return pl.pallas_call(
        functools.partial(_expert_kernel, tm, n),
        out_shape=jax.ShapeDtypeStruct((n * TOP_K, dh), jnp.uint32),
        grid_spec=grid_spec,
        compiler_params=_params(56, 2),
        name="experts",
    )(block_expert, block_nvalid, nused, slots, hpk, wgu, wgu, wd, bgu, bgu, bd)


def _combine_kernel(x_ref, gate_ref, y0_ref, y1_ref, y2_ref, y3_ref, g_ref, o_ref):
    d = x_ref.shape[1]
    dh = d // 2
    lo_acc, hi_acc = x_ref[:, :dh], x_ref[:, dh:]
    for k, y_ref in enumerate((y0_ref, y1_ref, y2_ref, y3_ref)):
        lo, hi = _unpack_bf16_pair(y_ref[...])
        gk = gate_ref[:, k:k + 1]
        lo_acc = lo_acc + gk * lo
        hi_acc = hi_acc + gk * hi
    ms = (jnp.sum(lo_acc * lo_acc, axis=-1, keepdims=True) + jnp.sum(hi_acc * hi_acc, axis=-1, keepdims=True)) / d
    r = lax.rsqrt(ms + RMS_EPS)
    o_ref[:, :dh] = lo_acc * r * g_ref[:, :dh]
    o_ref[:, dh:] = hi_acc * r * g_ref[:, dh:]


def _combine(x1, gate, y, g, row0, rows, n_tok, tm=256):
    d = x1.shape[1]
    tm = min(tm, rows)
    b0 = row0 // tm
    nbt = n_tok // tm
    ys = [pl.BlockSpec((tm, d // 2), functools.partial(lambda i, k: (k * nbt + b0 + i, 0), k=k))
          for k in range(TOP_K)]
    return pl.pallas_call(
        _combine_kernel,
        out_shape=jax.ShapeDtypeStruct((rows, d), F32),
        grid=(rows // tm,),
        in_specs=[pl.BlockSpec((tm, d), lambda i: (b0 + i, 0)),
                  pl.BlockSpec((tm, LANES), lambda i: (b0 + i, 0))] + ys +
                 [pl.BlockSpec((1, d), lambda i: (0, 0))],
        out_specs=pl.BlockSpec((tm, d), lambda i: (i, 0)),
        compiler_params=_params(48, 1),
        name="combine",
    )(x1, gate, y, y, y, y, g)


def _gate_weight_tables(w_g, b_g, H):
    d = w_g.shape[0]
    wi = jnp.stack([w_g[:, 0:H], w_g[:, 2 * H:3 * H]], axis=-1)
    wf = jnp.stack([w_g[:, H:2 * H], w_g[:, 3 * H:4 * H]], axis=-1)
    bi = jnp.stack([b_g[0:H], b_g[2 * H:3 * H]], axis=-1)
    bf = jnp.stack([b_g[H:2 * H], b_g[3 * H:4 * H]], axis=-1)

    def col(w):
        w6 = jnp.repeat(w, 3, axis=-1)
        w16 = jnp.pad(w6, [(0, 0)] * (w6.ndim - 1) + [(0, 10)]).reshape(w.shape[:-2] + (H * 16,))
        return jnp.pad(w16, [(0, 0)] * (w16.ndim - 1) + [(0, LANES - H * 16)])

    def rowt(w):
        return jnp.pad(w, ((0, 16 - 2 * H), (0, 0)))

    wit = rowt(wi.reshape(d, 2 * H).T)
    wft = rowt(wf.reshape(d, 2 * H).T)
    return (col(wi).astype(BF16), col(wf).astype(BF16), col(bi)[None].astype(F32), col(bf)[None].astype(F32),
            wit.astype(BF16), wft.astype(BF16), rowt(bi.reshape(2 * H, 1)).astype(F32),
            rowt(bf.reshape(2 * H, 1)).astype(F32))


def _mlstm_schedule(seq_lens, L):
    fwd, bwd, reset, base = [], [], [], 0
    for t in seq_lens:
        nc = t // L
        for c in range(nc):
            fwd.append(base + c)
            bwd.append(base + nc - 1 - c)
            reset.append(1 if c == 0 else 0)
        base += nc
    as_i32 = lambda v: jnp.asarray(v, dtype=jnp.int32)
    return as_i32(fwd), as_i32(bwd), as_i32(reset)


def kernel(x_prompt, x_sample, g_mix, w_in, b_gates, g_mlstm_out, g_na_out, rpb, w_out, g_ffn, w_router,
           b_router, w_gate_up, b_gate_up, w_down, b_down, g_final):
    assert g_mix.shape[0] == 1, "single layer"
    bp, tp, d = x_prompt.shape
    bs, ts, _ = x_sample.shape
    np_, ns = bp * tp, bs * ts
    n = np_ + ns
    H = b_gates.shape[1] // 4
    na_heads = rpb.shape[1]
    wm = g_mlstm_out.shape[1]
    wa = g_na_out.shape[1]
    dv = wm // H
    dk = dv // 2
    da = wa // na_heads
    E = w_router.shape[2]
    F = w_down.shape[2]
    L = min(MLSTM_CHUNK, tp, ts)
    assert H <= 8 and dk == LANES and da == LANES and E <= LANES and np_ % ts == 0

    xp = x_prompt.reshape(np_, d)
    xs = x_sample.reshape(ns, d)

    w = w_in[0]
    o0 = 2 * H * dk + 2 * wm
    w_main = jnp.concatenate([w[:, :o0], w[:, o0 + 4 * H:]], axis=1).astype(BF16)
    gate_tabs = _gate_weight_tables(w[:, o0:o0 + 4 * H], b_gates[0], H)
    q_off, k_off, v_off, om_off = 0, H * dk, 2 * H * dk, 2 * H * dk + wm
    qa_off, ka_off, va_off = o0, o0 + wa, o0 + 2 * wa

    h = _norm_in(xp, xs, g_mix)
    proj = _matmul(h, w_main)
    col, row = _gates(h, *gate_tabs, L)
    sched = _mlstm_schedule([tp] * bp + [ts] * bs, L)
    hf, hb = _mlstm(proj, col, row, sched, H, L, dk, dv, q_off, k_off, v_off)
    bias = _na_bias(rpb[0])
    hap = _na(proj, bias, g_na_out, bp, 0, tp, na_heads, da, qa_off, ka_off, va_off)
    has_ = _na(proj, bias, g_na_out, bs, np_, ts, na_heads, da, qa_off, ka_off, va_off)
    x1 = _out_proj(hf, hb, proj, om_off, hap, has_, g_mlstm_out, w_out[0].astype(BF16), xp, xs, H, dv)

    wr = jnp.pad(w_router[0], ((0, 0), (0, LANES - E)))
    wr_hi = wr.astype(BF16)
    wr_lo = (wr - wr_hi.astype(F32)).astype(BF16)
    br = jnp.pad(b_router, ((0, 0), (0, LANES - E)))
    idx, gate, rank, cnt, hpk = _router(x1, g_ffn, wr_hi, wr_lo, br, E)
    idx, rank = idx[:, :TOP_K], rank[:, :TOP_K]

    tm = 512 if (n * TOP_K) % 512 == 0 and n * TOP_K >= 512 * E else 256
    tf = min(512, F)
    nb = n * TOP_K // tm + E
    counts = cnt[0, :E].astype(jnp.int32)
    nblk = (counts + tm - 1) // tm
    pend = jnp.cumsum(nblk) * tm
    pstart = pend - nblk * tm
    dest = (pstart[idx] + rank).reshape(-1)
    slots = jnp.zeros((nb * tm,), jnp.int32).at[dest].set(jnp.arange(n * TOP_K, dtype=jnp.int32),
                                                           unique_indices=True).reshape(nb, tm)
    blk0 = jnp.arange(nb, dtype=jnp.int32) * tm
    block_expert = jnp.minimum(jnp.searchsorted(pend, blk0, side='right'), E - 1).astype(jnp.int32)
    block_nvalid = jnp.clip(counts[block_expert] - (blk0 - pstart[block_expert]), 0, tm).astype(jnp.int32)
    nused = (pend[-1:] // tm).astype(jnp.int32)

    y = _experts(block_expert, block_nvalid, nused, slots, hpk, w_gate_up[0].astype(BF16),
                 w_down[0].astype(BF16), b_gate_up[0][:, None, :], b_down[0][:, None, :], tm, tf)

    gf = g_final[None, :]
    y_prompt = _combine(x1, gate, y, gf, 0, np_, n).reshape(bp, tp, d)
    y_sample = _combine(x1, gate, y, gf, np_, ns, n).reshape(bs, ts, d)
    return (y_prompt, y_sample)
```

```python
import functools

import jax
import jax.numpy as jnp
from jax import lax
from jax.experimental import pallas as pl
from jax.experimental.pallas import tpu as pltpu

GRID_W = 64
GATE_SOFTCAP = 15.0
NA_WIN_ROWS = 8
NA_WIN_COLS = 16
TOP_K = 4
SWIGLU_LIMIT = 7.0
SWIGLU_ALPHA = 1.702
RMS_EPS = 1e-6

LANES = 128
SUBLANES = 8
MLSTM_CHUNK = 256
NA_ROWS_PER_STEP = 8
NEG_BIG = -1e30
MIB = 1024 * 1024

F32 = jnp.float32
BF16 = jnp.bfloat16


def _params(vmem_mib, n_axes):
    return pltpu.CompilerParams(
        dimension_semantics=("arbitrary",) * n_axes, vmem_limit_bytes=vmem_mib * MIB)


def _dot(a, b):
    return jnp.dot(a, b, preferred_element_type=F32)


def _dot_nt(a, b):
    return lax.dot_general(a, b, (((1,), (1,)), ((), ())), preferred_element_type=F32)


def _dot_tn(a, b):
    return lax.dot_general(a, b, (((0,), (0,)), ((), ())), preferred_element_type=F32)


def _split3(x):
    hi = x.astype(BF16)
    r1 = x - hi.astype(F32)
    mid = r1.astype(BF16)
    lo = (r1 - mid.astype(F32)).astype(BF16)
    return hi, mid, lo


def _rms(x, g):
    return x * lax.rsqrt(jnp.mean(x * x, axis=-1, keepdims=True) + RMS_EPS) * g


def _norm_in_kernel(nbp, xp_ref, xs_ref, g_ref, o_ref):
    i = pl.program_id(0)
    x = jnp.where(i < nbp, xp_ref[...], xs_ref[...])
    o_ref[...] = _rms(x, g_ref[...]).astype(o_ref.dtype)


def _norm_in(xp, xs, g, tm=256):
    np_, d = xp.shape
    ns = xs.shape[0]
    nbp, nbs = np_ // tm, ns // tm
    return pl.pallas_call(
        functools.partial(_norm_in_kernel, nbp),
        out_shape=jax.ShapeDtypeStruct((np_ + ns, d), BF16),
        grid=(nbp + nbs,),
        in_specs=[
            pl.BlockSpec((tm, d), lambda i: (jnp.minimum(i, nbp - 1), 0)),
            pl.BlockSpec((tm, d), lambda i: (jnp.maximum(i - nbp, 0), 0)),
            pl.BlockSpec((1, d), lambda i: (0, 0)),
        ],
        out_specs=pl.BlockSpec((tm, d), lambda i: (i, 0)),
        compiler_params=_params(48, 1),
        name="norm_in",
    )(xp, xs, g)


def _matmul_kernel(a_ref, b_ref, o_ref):
    o_ref[...] = _dot(a_ref[...], b_ref[...]).astype(o_ref.dtype)


def _matmul(a, b, tm=1024, tn=1024):
    m, k = a.shape
    n = b.shape[1]
    tm, tn = min(tm, m), min(tn, n)
    return pl.pallas_call(
        _matmul_kernel,
        out_shape=jax.ShapeDtypeStruct((m, n), BF16),
        grid=(m // tm, n // tn),
        in_specs=[pl.BlockSpec((tm, k), lambda i, j: (i, 0)),
                  pl.BlockSpec((k, tn), lambda i, j: (0, j))],
        out_specs=pl.BlockSpec((tm, tn), lambda i, j: (i, j)),
        compiler_params=_params(48, 2),
        name="in_proj",
    )(a, b)


def _softcap(x):
    return GATE_SOFTCAP * jnp.tanh(x / GATE_SOFTCAP)


def _log_sigmoid(x):
    return jnp.minimum(x, 0.0) - jnp.log1p(jnp.exp(-jnp.abs(x)))


def _gates_kernel(L, h_ref, wi_ref, wf_ref, bi_ref, bf_ref, wit_ref, wft_ref, bit_ref, bft_ref,
                  col_ref, row_ref):
    h = h_ref[...]
    t = h.shape[0]
    gi = _softcap(_dot(h, wi_ref[...]) + bi_ref[...])
    lf = _log_sigmoid(_softcap(_dot(h, wf_ref[...]) + bf_ref[...]))
    git = _softcap(_dot_nt(wit_ref[...], h) + bit_ref[...])
    lft = _log_sigmoid(_softcap(_dot_nt(wft_ref[...], h) + bft_ref[...]))

    r = lax.broadcasted_iota(jnp.int32, (L, L), 0)
    c = lax.broadcasted_iota(jnp.int32, (L, L), 1)
    tri = (c <= r).astype(BF16)
    trit = (r <= c).astype(BF16)

    lane = lax.broadcasted_iota(jnp.int32, (L, LANES), 1)
    q = lane % 16
    is_bwd = q >= 3
    kind = jnp.where(is_bwd, q - 3, q)
    rowi = lax.broadcasted_iota(jnp.int32, (16, L), 0)
    row_bwd = (rowi % 2) == 1

    for ci in range(t // L):
        sl = slice(ci * L, (ci + 1) * L)
        lfc, gic = lf[sl], gi[sl]
        hi, mid, lo = _split3(lfc)
        b = _dot(tri, hi) + _dot(tri, mid) + _dot(tri, lo)
        a = b - lfc
        tot = b[L - 1:L, :]
        alpha = jnp.where(is_bwd, -a, b)
        gamma = jnp.where(is_bwd, tot - a, b)
        omega = jnp.where(is_bwd, a + gic, tot - b + gic)
        col_ref[sl, :] = jnp.where(kind == 0, alpha, jnp.where(kind == 1, gamma, omega))

        lfct, gict = lft[:, sl], git[:, sl]
        hi, mid, lo = _split3(lfct)
        bt = _dot(hi, trit) + _dot(mid, trit) + _dot(lo, trit)
        at = bt - lfct
        row_ref[:, sl] = jnp.where(row_bwd, at + gict, gict - bt)


def _gates(h, wi, wf, bi, bf, wit, wft, bit, bft, L, tm=1024):
    n, d = h.shape
    tm = min(tm, n)
    full = lambda i: (0, 0)
    return pl.pallas_call(
        functools.partial(_gates_kernel, L),
        out_shape=(jax.ShapeDtypeStruct((n, LANES), F32), jax.ShapeDtypeStruct((16, n), F32)),
        grid=(n // tm,),
        in_specs=[pl.BlockSpec((tm, d), lambda i: (i, 0)),
                  pl.BlockSpec((d, LANES), full), pl.BlockSpec((d, LANES), full),
                  pl.BlockSpec((1, LANES), full), pl.BlockSpec((1, LANES), full),
                  pl.BlockSpec((16, d), full), pl.BlockSpec((16, d), full),
                  pl.BlockSpec((16, 1), full), pl.BlockSpec((16, 1), full)],
        out_specs=(pl.BlockSpec((tm, LANES), lambda i: (i, 0)),
                   pl.BlockSpec((16, tm), lambda i: (0, i))),
        compiler_params=_params(40, 1),
        name="mlstm_gates",
    )(h, wi, wf, bi, bf, wit, wft, bit, bft)


def _mlstm_kernel(H, L, dk, dv, fwd_ref, bwd_ref, reset_ref,
                  qf_ref, kf_ref, vf_ref, colf_ref, rowf_ref,
                  qb_ref, kb_ref, vb_ref, colb_ref, rowb_ref,
                  hf_ref, hb_ref, ct_ref, nr_ref):
    s = pl.program_id(0)

    @pl.when(reset_ref[s] == 1)
    def _():
        ct_ref[...] = jnp.zeros_like(ct_ref)
        nr_ref[...] = jnp.zeros_like(nr_ref)

    r = lax.broadcasted_iota(jnp.int32, (L, L), 0)
    c = lax.broadcasted_iota(jnp.int32, (L, L), 1)
    masks = (c <= r, c >= r)
    ones = jnp.ones((L, LANES), BF16)
    scale = dk ** -0.5
    dirs = ((qf_ref, kf_ref, vf_ref, colf_ref, rowf_ref, hf_ref),
            (qb_ref, kb_ref, vb_ref, colb_ref, rowb_ref, hb_ref))

    for hd in range(H):
        for d, (q_ref, k_ref, v_ref, col_ref, row_ref, o_ref) in enumerate(dirs):
            ch = hd * 2 + d
            q = q_ref[:, hd * dk:(hd + 1) * dk]
            k = k_ref[:, hd * dk:(hd + 1) * dk]
            v = v_ref[:, hd * dv:(hd + 1) * dv]
            lane0 = hd * 16 + d * 3
            alpha = col_ref[:, lane0:lane0 + 1]
            gamma = col_ref[:, lane0 + 1:lane0 + 2]
            omega = col_ref[:, lane0 + 2:lane0 + 3]
            beta = row_ref[ch:ch + 1, :]
            g_end = gamma[L - 1:L, :] if d == 0 else gamma[0:1, :]

            logd = jnp.where(masks[d], alpha + beta, -jnp.inf)
            p = (_dot_nt(q, k) * scale) * jnp.exp(logd)
            pb = p.astype(BF16)
            ct = ct_ref[ch]
            nr = nr_ref[ch]
            eg = jnp.exp(gamma) * scale
            num = _dot(pb, v) + eg * _dot(q, ct.astype(BF16))
            den = _dot(pb, ones) + eg * _dot(q, nr.astype(BF16))
            inv = 1.0 / jnp.maximum(jnp.abs(den), 1.0)
            o_ref[:, hd * dv:(hd + 1) * dv] = (
                num * jnp.concatenate([inv] * (dv // LANES), axis=1)).astype(o_ref.dtype)

            kw = (k.astype(F32) * jnp.exp(omega)).astype(BF16)
            decay = jnp.exp(g_end)
            ct_ref[ch] = decay * ct + _dot_tn(kw, v)
            nr_ref[ch] = decay * nr + _dot_tn(kw, ones)


def _mlstm(proj, col, row, sched, H, L, dk, dv, q_off, k_off, v_off):
    n = proj.shape[0]
    fwd, bwd, reset = sched
    steps = fwd.shape[0]
    qw, vw = H * dk, H * dv
    qb, kb, vb = q_off // qw, k_off // qw, v_off // vw
    fi = lambda s, f, b, r: f[s]
    bi = lambda s, f, b, r: b[s]

    def specs(pick):
        return [pl.BlockSpec((L, qw), lambda s, f, b, r: (pick(s, f, b, r), qb)),
                pl.BlockSpec((L, qw), lambda s, f, b, r: (pick(s, f, b, r), kb)),
                pl.BlockSpec((L, vw), lambda s, f, b, r: (pick(s, f, b, r), vb)),
                pl.BlockSpec((L, LANES), lambda s, f, b, r: (pick(s, f, b, r), 0)),
                pl.BlockSpec((16, L), lambda s, f, b, r: (0, pick(s, f, b, r)))]

    grid_spec = pltpu.PrefetchScalarGridSpec(
        num_scalar_prefetch=3, grid=(steps,),
        in_specs=specs(fi) + specs(bi),
        out_specs=(pl.BlockSpec((L, vw), lambda s, f, b, r: (f[s], 0)),
                   pl.BlockSpec((L, vw), lambda s, f, b, r: (b[s], 0))),
        scratch_shapes=[pltpu.VMEM((2 * H, dk, dv), F32), pltpu.VMEM((2 * H, dk, LANES), F32)])
    return pl.pallas_call(
        functools.partial(_mlstm_kernel, H, L, dk, dv),
        out_shape=(jax.ShapeDtypeStruct((n, vw), BF16), jax.ShapeDtypeStruct((n, vw), BF16)),
        grid_spec=grid_spec,
        compiler_params=_params(40, 1),
        name="mlstm",
    )(fwd, bwd, reset, proj, proj, proj, col, row, proj, proj, proj, col, row)


def _na_kernel(rows, d, q_ref, k_ref, v_ref, bias_ref, g_ref, o_ref):
    scale = d ** -0.5
    win = NA_WIN_ROWS * GRID_W
    nq = NA_ROWS_PER_STEP * GRID_W
    g = g_ref[...]
    ones = jnp.ones((win, LANES), BF16)

    def block(rb, carry):
        starts, logits = [], []
        for u in range(NA_ROWS_PER_STEP):
            r = rb * NA_ROWS_PER_STEP + u
            rs = jnp.clip(r - NA_WIN_ROWS // 2, 0, rows - NA_WIN_ROWS)
            ks = pl.multiple_of(rs * GRID_W, GRID_W)
            q = q_ref[pl.ds(pl.multiple_of(r * GRID_W, GRID_W), GRID_W), :]
            logits.append(_dot_nt(q, k_ref[pl.ds(ks, win), :]) * scale + bias_ref[r - rs])
            starts.append(ks)
        s = jnp.concatenate(logits, axis=0)
        pb = jnp.exp(s - jnp.max(s, axis=-1, keepdims=True)).astype(BF16)
        l = _dot(pb, ones)
        o = jnp.concatenate(
            [_dot(pb[u * GRID_W:(u + 1) * GRID_W], v_ref[pl.ds(starts[u], win), :])
             for u in range(NA_ROWS_PER_STEP)], axis=0) / l
        o_ref[pl.ds(pl.multiple_of(rb * nq, nq), nq), :] = _rms(o, g).astype(o_ref.dtype)
        return carry

    lax.fori_loop(0, rows // NA_ROWS_PER_STEP, block, 0)


def _na(proj, bias, g, nseq, row0, T, heads, d, q_off, k_off, v_off):
    rows = T // GRID_W
    sb = row0 // T
    qb, kb, vb = q_off // d, k_off // d, v_off // d
    return pl.pallas_call(
        functools.partial(_na_kernel, rows, d),
        out_shape=jax.ShapeDtypeStruct((nseq * T, heads * d), BF16),
        grid=(nseq, heads),
        in_specs=[pl.BlockSpec((T, d), lambda b, h: (sb + b, qb + h)),
                  pl.BlockSpec((T, d), lambda b, h: (sb + b, kb + h)),
                  pl.BlockSpec((T, d), lambda b, h: (sb + b, vb + h)),
                  pl.BlockSpec((None, NA_WIN_ROWS, GRID_W, NA_WIN_ROWS * GRID_W), lambda b, h: (h, 0, 0, 0)),
                  pl.BlockSpec((1, d), lambda b, h: (0, h))],
        out_specs=pl.BlockSpec((T, d), lambda b, h: (b, h)),
        compiler_params=_params(48, 2),
        name="natten",
    )(proj, proj, proj, bias, g)


def _na_bias(rpb):
    heads = rpb.shape[0]
    ncol = 2 * NA_WIN_COLS - 1
    cq = jnp.arange(GRID_W)
    cs = jnp.clip(cq - NA_WIN_COLS // 2, 0, GRID_W - NA_WIN_COLS)
    j = jnp.arange(GRID_W)
    inside = (j[None, :] >= cs[:, None]) & (j[None, :] < cs[:, None] + NA_WIN_COLS)
    coff = j[None, :] - cq[:, None] + (NA_WIN_COLS - 1)
    pick = ((coff[None] == jnp.arange(ncol)[:, None, None]) & inside[None]).astype(F32)
    rsel = jnp.stack([rpb.astype(F32)[:, NA_WIN_ROWS - 1 - var:2 * NA_WIN_ROWS - 1 - var, :]
                      for var in range(NA_WIN_ROWS)], axis=1)
    tab = jnp.einsum('hvic,cqj->hvqij', rsel, pick, precision=lax.Precision.HIGHEST)
    tab = jnp.where(inside[None, None, :, None, :], tab, NEG_BIG)
    return tab.reshape(heads, NA_WIN_ROWS, GRID_W, NA_WIN_ROWS * GRID_W)


def _out_proj_kernel(nbp, H, dv, hf_ref, hb_ref, om_ref, hap_ref, has_ref, g_ref, w_ref,
                     xp_ref, xs_ref, o_ref, lhs_ref):
    i = pl.program_id(0)
    j = pl.program_id(1)
    wm = H * dv

    @pl.when(j == 0)
    def _():
        for hd in range(H):
            sl = slice(hd * dv, (hd + 1) * dv)
            hs = hf_ref[:, sl].astype(F32) + hb_ref[:, sl].astype(F32)
            y = _rms(hs, g_ref[:, sl]) * jax.nn.sigmoid(om_ref[:, sl].astype(F32))
            lhs_ref[:, sl] = y.astype(BF16)
        lhs_ref[:, wm:] = jnp.where(i < nbp, hap_ref[...], has_ref[...])

    x = jnp.where(i < nbp, xp_ref[...], xs_ref[...])
    o_ref[...] = x + _dot(lhs_ref[...], w_ref[...])


def _out_proj(hf, hb, proj, om_off, hap, has_, g_ml, w_out, xp, xs, H, dv, tm=512, tn=512):
    n, wm = hf.shape
    wa = hap.shape[1]
    d = w_out.shape[1]
    tm, tn = min(tm, xp.shape[0], xs.shape[0]), min(tn, d)
    nbp = xp.shape[0] // tm
    omb = om_off // wm
    pidx = lambda i, j: (jnp.minimum(i, nbp - 1), 0)
    sidx = lambda i, j: (jnp.maximum(i - nbp, 0), 0)
    return pl.pallas_call(
        functools.partial(_out_proj_kernel, nbp, H, dv),
        out_shape=jax.ShapeDtypeStruct((n, d), F32),
        grid=(n // tm, d // tn),
        in_specs=[pl.BlockSpec((tm, wm), lambda i, j: (i, 0)),
                  pl.BlockSpec((tm, wm), lambda i, j: (i, 0)),
                  pl.BlockSpec((tm, wm), lambda i, j: (i, omb)),
                  pl.BlockSpec((tm, wa), pidx),
                  pl.BlockSpec((tm, wa), sidx),
                  pl.BlockSpec((1, wm), lambda i, j: (0, 0)),
                  pl.BlockSpec((wm + wa, tn), lambda i, j: (0, j)),
                  pl.BlockSpec((tm, tn), lambda i, j: (jnp.minimum(i, nbp - 1), j)),
                  pl.BlockSpec((tm, tn), lambda i, j: (jnp.maximum(i - nbp, 0), j))],
        out_specs=pl.BlockSpec((tm, tn), lambda i, j: (i, j)),
        scratch_shapes=[pltpu.VMEM((tm, wm + wa), BF16)],
        compiler_params=_params(48, 2),
        name="out_proj",
    )(hf, hb, proj, hap, has_, g_ml, w_out, xp, xs)


def _pack_bf16_pair(lo, hi):
    lo_bits = lax.bitcast_convert_type(lo.astype(BF16).astype(F32), jnp.uint32)
    hi_bits = lax.bitcast_convert_type(hi.astype(BF16).astype(F32), jnp.uint32)
    return hi_bits | (lo_bits >> 16)


def _unpack_bf16_pair(w):
    lo = lax.bitcast_convert_type(w << 16, F32)
    hi = lax.bitcast_convert_type(w & jnp.uint32(0xFFFF0000), F32)
    return lo, hi


def _router_kernel(E, x_ref, g_ref, whi_ref, wlo_ref, b_ref, idx_ref, gate_ref, rank_ref, cnt_ref, hpk_ref,
                   carry_ref):
    i = pl.program_id(0)

    @pl.when(i == 0)
    def _():
        carry_ref[...] = jnp.zeros_like(carry_ref)

    h = _rms(x_ref[...], g_ref[...])
    tm, d = h.shape
    hpk_ref[...] = _pack_bf16_pair(h[:, :d // 2], h[:, d // 2:])
    hhi = h.astype(BF16)
    hlo = (h - hhi.astype(F32)).astype(BF16)
    whi = whi_ref[...]
    logits = _dot(hhi, whi) + _dot(hlo, whi) + _dot(hhi, wlo_ref[...]) + b_ref[...]

    lane = lax.broadcasted_iota(jnp.int32, (tm, LANES), 1)
    work = jnp.where(lane < E, logits, NEG_BIG)
    vals, idxs = [], []
    for _ in range(TOP_K):
        m = jnp.max(work, axis=-1, keepdims=True)
        am = jnp.min(jnp.where(work == m, lane, LANES), axis=-1, keepdims=True)
        vals.append(m)
        idxs.append(am)
        work = jnp.where(lane == am, -jnp.inf, work)
    es = [jnp.exp(v - vals[0]) for v in vals]
    denom = es[0] + es[1] + es[2] + es[3]

    sel = (work == -jnp.inf).astype(BF16)
    r = lax.broadcasted_iota(jnp.int32, (tm, tm), 0)
    c = lax.broadcasted_iota(jnp.int32, (tm, tm), 1)
    before = _dot((c < r).astype(BF16), sel) + carry_ref[0:1, :]
    carry_ref[...] = carry_ref[...] + jnp.sum(sel.astype(F32), axis=0, keepdims=True)
    cnt_ref[...] = carry_ref[...]

    idx_o = jnp.zeros((tm, LANES), jnp.int32)
    gate_o = jnp.zeros((tm, LANES), F32)
    rank_o = jnp.zeros((tm, LANES), jnp.int32)
    for k in range(TOP_K):
        rk = jnp.sum(jnp.where(lane == idxs[k], before, 0.0), axis=-1, keepdims=True)
        idx_o = jnp.where(lane == k, idxs[k], idx_o)
        gate_o = jnp.where(lane == k, es[k] / denom, gate_o)
        rank_o = jnp.where(lane == k, rk.astype(jnp.int32), rank_o)
    idx_ref[...] = idx_o
    gate_ref[...] = gate_o
    rank_ref[...] = rank_o


def _router(x1, g, whi, wlo, b, E, tm=256):
    n, d = x1.shape
    tm = min(tm, n)
    full = lambda i: (0, 0)
    tok = pl.BlockSpec((tm, LANES), lambda i: (i, 0))
    return pl.pallas_call(
        functools.partial(_router_kernel, E),
        out_shape=(jax.ShapeDtypeStruct((n, LANES), jnp.int32), jax.ShapeDtypeStruct((n, LANES), F32),
                   jax.ShapeDtypeStruct((n, LANES), jnp.int32), jax.ShapeDtypeStruct((8, LANES), F32),
                   jax.ShapeDtypeStruct((n, d // 2), jnp.uint32)),
        grid=(n // tm,),
        in_specs=[pl.BlockSpec((tm, d), lambda i: (i, 0)), pl.BlockSpec((1, d), full),
                  pl.BlockSpec((d, LANES), full), pl.BlockSpec((d, LANES), full),
                  pl.BlockSpec((1, LANES), full)],
        out_specs=(tok, tok, tok, pl.BlockSpec((8, LANES), full), pl.BlockSpec((tm, d // 2), lambda i: (i, 0))),
        scratch_shapes=[pltpu.VMEM((8, LANES), F32)],
        compiler_params=_params(40, 1),
        name="router",
    )(x1, g, whi, wlo, b)


def _expert_kernel(tm, n_tok, nf_static, be_ref, nv_ref, nused_ref, slots_hbm, hpk_hbm, wg_ref, wu_ref, wd_ref,
                   bg_ref, bu_ref, bd_ref, y_hbm, idx_smem, gbuf_ref, xb_ref, acc_ref, ybuf_ref,
                   sem_idx, sem_in, sem_out):
    i = pl.program_id(0)
    f = pl.program_id(1)
    nf = nf_static
    nused = nused_ref[0]
    dh = gbuf_ref.shape[2]
    ngrp = tm // SUBLANES
    spare_row = slots_hbm.shape[0] - 1


    def idx_base(ring):
        return pl.multiple_of(ring * (2 * tm), 2 * tm)

    def idx_fetch(row, ring):
        return pltpu.make_async_copy(slots_hbm.at[row, 0], idx_smem.at[pl.ds(idx_base(ring), 2 * tm)], sem_idx)

    def in_row(tok, gi, u):
        return pltpu.make_async_copy(hpk_hbm.at[pl.ds(tok, 1), :], gbuf_ref.at[gi, pl.ds(u, 1), :], sem_in)

    def gather_start(ring):
        base = idx_base(ring)

        def group(gi, carry):
            for u in range(SUBLANES):
                in_row(idx_smem[base + gi * SUBLANES + u], gi, u).start()
            return carry

        lax.fori_loop(0, ngrp, group, 0)

    def gather_wait():
        def group(gi, carry):
            pltpu.make_async_copy(hpk_hbm.at[pl.ds(0, SUBLANES), :], gbuf_ref.at[0], sem_in).wait()
            return carry

        lax.fori_loop(0, ngrp, group, 0)

    def out_row(gi, u, dst):
        return pltpu.make_async_copy(ybuf_ref.at[gi, pl.ds(u, 1), :], y_hbm.at[pl.ds(dst, 1), :], sem_out)

    def out_group():
        return pltpu.make_async_copy(ybuf_ref.at[0], y_hbm.at[pl.ds(0, SUBLANES), :], sem_out)

    def scatter_wait_all():
        def group(gi, carry):
            out_group().wait()
            return carry

        lax.fori_loop(0, ngrp, group, 0)

    def scatter_start(ring, nv):
        base = idx_base(ring) + tm

        def dst_row(s):
            return idx_smem[base + s]

        def group(gi, carry):
            for u in range(SUBLANES):
                out_row(gi, u, dst_row(gi * SUBLANES + u)).start()
            return carry

        def single(s, carry):
            out_row(lax.shift_right_logical(s, 3), s & (SUBLANES - 1), dst_row(s)).start()
            return carry

        n8 = lax.shift_right_logical(nv, 3)
        lax.fori_loop(0, n8, group, 0)
        lax.fori_loop(n8 * SUBLANES, nv, single, 0)

    def scatter_wait(nv):
        def group(gi, carry):
            out_group().wait()
            return carry

        def single(s, carry):
            out_row(0, 0, 0).wait()
            return carry

        n8 = lax.shift_right_logical(nv, 3)
        lax.fori_loop(0, n8, group, 0)
        lax.fori_loop(n8 * SUBLANES, nv, single, 0)

    @pl.when(i < nused)
    def _():
        nv = nv_ref[i]
        ring_cur = lax.rem(i, 3)
        ring_next = lax.rem(i + 1, 3)
        ring_prev = lax.rem(i + 2, 3)

        @pl.when(f == 0)
        def _():
            @pl.when(i == 0)
            def _():
                for row, ring in ((0, 0), (spare_row, 2)):
                    cp = idx_fetch(row, ring)
                    cp.start()
                    cp.wait()
                gather_start(0)
                ybuf_ref[...] = jnp.zeros_like(ybuf_ref)

            idx_fetch(i + 1, ring_next).start()
            gather_wait()
            lo, hi = _unpack_bf16_pair(gbuf_ref[...].reshape(tm, dh))
            xb_ref[:, :dh] = lo.astype(BF16)
            xb_ref[:, dh:] = hi.astype(BF16)
            acc_ref[...] = jnp.broadcast_to(bd_ref[0], acc_ref.shape)
            idx_fetch(i + 1, ring_next).wait()

        def neighbour_dma_part():
            gbase = idx_base(ring_next)
            sbase = idx_base(ring_prev) + tm
            gps = ngrp // nf
            for j in range(gps):
                gi = f * gps + j
                for u in range(SUBLANES):
                    s = gi * SUBLANES + u
                    in_row(idx_smem[gbase + s], gi, u).start()
                    out_row(gi, u, idx_smem[sbase + s]).start()

        def ffn(rows):
            neighbour_dma_part()
            x = xb_ref[:rows, :]
            gg = jnp.minimum(_dot(x, wg_ref[0]) + bg_ref[0], SWIGLU_LIMIT)
            uu = jnp.clip(_dot(x, wu_ref[0]) + bu_ref[0], -SWIGLU_LIMIT, SWIGLU_LIMIT)
            act = ((uu + 1.0) * (gg * jax.nn.sigmoid(gg * SWIGLU_ALPHA))).astype(BF16)
            acc_ref[:rows, :] += _dot(act, wd_ref[0])

        @pl.when(nv > tm // 2)
        def _():
            ffn(tm)

        @pl.when(nv <= tm // 2)
        def _():
            ffn(tm // 2)

        @pl.when(f == nf - 1)
        def _():
            scatter_wait_all()
            y = acc_ref[...]
            ybuf_ref[...] = _pack_bf16_pair(y[:, :dh], y[:, dh:]).reshape(ngrp, SUBLANES, dh)

            @pl.when(i == nused - 1)
            def _():
                scatter_start(ring_cur, nv)
                scatter_wait(nv)
                gather_wait()


def _experts(block_expert, block_nvalid, nused, slots, hpk, wgu, wd, bgu, bd, tm, tf):
    n, dh = hpk.shape
    d = 2 * dh
    E, _, f2 = wgu.shape
    F = f2 // 2
    nb = slots.shape[0] - 1
    nf = F // tf
    assert (tm // SUBLANES) % nf == 0

    def fidx(i, f, nu):
        return jnp.where(i < nu[0], f, nf - 1)

    def bidx(i, be, nu):
        return be[jnp.minimum(i, nu[0] - 1)]

    grid_spec = pltpu.PrefetchScalarGridSpec(
        num_scalar_prefetch=3, grid=(nb, nf),
        in_specs=[pl.BlockSpec(memory_space=pl.ANY), pl.BlockSpec(memory_space=pl.ANY),
                  pl.BlockSpec((1, d, tf), lambda i, f, be, nv, nu: (bidx(i, be, nu), 0, fidx(i, f, nu))),
                  pl.BlockSpec((1, d, tf), lambda i, f, be, nv, nu: (bidx(i, be, nu), 0, nf + fidx(i, f, nu))),
                  pl.BlockSpec((1, tf, d), lambda i, f, be, nv, nu: (bidx(i, be, nu), fidx(i, f, nu), 0)),
                  pl.BlockSpec((1, 1, tf), lambda i, f, be, nv, nu: (bidx(i, be, nu), 0, fidx(i, f, nu))),
                  pl.BlockSpec((1, 1, tf), lambda i, f, be, nv, nu: (bidx(i, be, nu), 0, nf + fidx(i, f, nu))),
                  pl.BlockSpec((1, 1, d), lambda i, f, be, nv, nu: (bidx(i, be, nu), 0, 0))],
        out_specs=pl.BlockSpec(memory_space=pl.ANY),
        scratch_shapes=[pltpu.SMEM((3 * 2 * tm,), jnp.int32), pltpu.VMEM((tm // SUBLANES, SUBLANES, dh), jnp.uint32),
                        pltpu.VMEM((tm, d), BF16), pltpu.VMEM((tm, d), F32), pltpu.VMEM((tm // SUBLANES, SUBLANES, dh), jnp.uint32),
                        pltpu.SemaphoreType.DMA, pltpu.SemaphoreType.DMA, pltpu.SemaphoreType.DMA])
    return pl.pallas_call(
        functools.partial(_expert_kernel, tm, n, nf),
        out_shape=jax.ShapeDtypeStruct((n * TOP_K + tm, dh), jnp.uint32),
        grid_spec=grid_spec,
        compiler_params=_params(56, 2),
        name="experts",
    )(block_expert, block_nvalid, nused, slots, hpk, wgu, wgu, wd, bgu, bgu, bd)


def _combine_kernel(x_ref, gate_ref, y0_ref, y1_ref, y2_ref, y3_ref, g_ref, o_ref):
    d = x_ref.shape[1]
    dh = d // 2
    lo_acc, hi_acc = x_ref[:, :dh], x_ref[:, dh:]
    for k, y_ref in enumerate((y0_ref, y1_ref, y2_ref, y3_ref)):
        lo, hi = _unpack_bf16_pair(y_ref[...])
        gk = gate_ref[:, k:k + 1]
        lo_acc = lo_acc + gk * lo
        hi_acc = hi_acc + gk * hi
    ms = (jnp.sum(lo_acc * lo_acc, axis=-1, keepdims=True) + jnp.sum(hi_acc * hi_acc, axis=-1, keepdims=True)) / d
    r = lax.rsqrt(ms + RMS_EPS)
    o_ref[:, :dh] = lo_acc * r * g_ref[:, :dh]
    o_ref[:, dh:] = hi_acc * r * g_ref[:, dh:]


def _combine(x1, gate, y, g, row0, rows, n_tok, tm=256):
    d = x1.shape[1]
    tm = min(tm, rows)
    b0 = row0 // tm
    nbt = n_tok // tm
    ys = [pl.BlockSpec((tm, d // 2), functools.partial(lambda i, k: (k * nbt + b0 + i, 0), k=k))
          for k in range(TOP_K)]
    return pl.pallas_call(
        _combine_kernel,
        out_shape=jax.ShapeDtypeStruct((rows, d), F32),
        grid=(rows // tm,),
        in_specs=[pl.BlockSpec((tm, d), lambda i: (b0 + i, 0)),
                  pl.BlockSpec((tm, LANES), lambda i: (b0 + i, 0))] + ys +
                 [pl.BlockSpec((1, d), lambda i: (0, 0))],
        out_specs=pl.BlockSpec((tm, d), lambda i: (i, 0)),
        compiler_params=_params(48, 1),
        name="combine",
    )(x1, gate, y, y, y, y, g)


def _gate_weight_tables(w_g, b_g, H):
    d = w_g.shape[0]
    wi = jnp.stack([w_g[:, 0:H], w_g[:, 2 * H:3 * H]], axis=-1)
    wf = jnp.stack([w_g[:, H:2 * H], w_g[:, 3 * H:4 * H]], axis=-1)
    bi = jnp.stack([b_g[0:H], b_g[2 * H:3 * H]], axis=-1)
    bf = jnp.stack([b_g[H:2 * H], b_g[3 * H:4 * H]], axis=-1)

    def col(w):
        w6 = jnp.repeat(w, 3, axis=-1)
        w16 = jnp.pad(w6, [(0, 0)] * (w6.ndim - 1) + [(0, 10)]).reshape(w.shape[:-2] + (H * 16,))
        return jnp.pad(w16, [(0, 0)] * (w16.ndim - 1) + [(0, LANES - H * 16)])

    def rowt(w):
        return jnp.pad(w, ((0, 16 - 2 * H), (0, 0)))

    wit = rowt(wi.reshape(d, 2 * H).T)
    wft = rowt(wf.reshape(d, 2 * H).T)
    return (col(wi).astype(BF16), col(wf).astype(BF16), col(bi)[None].astype(F32), col(bf)[None].astype(F32),
            wit.astype(BF16), wft.astype(BF16), rowt(bi.reshape(2 * H, 1)).astype(F32),
            rowt(bf.reshape(2 * H, 1)).astype(F32))


def _mlstm_schedule(seq_lens, L):
    fwd, bwd, reset, base = [], [], [], 0
    for t in seq_lens:
        nc = t // L
        for c in range(nc):
            fwd.append(base + c)
            bwd.append(base + nc - 1 - c)
            reset.append(1 if c == 0 else 0)
        base += nc
    as_i32 = lambda v: jnp.asarray(v, dtype=jnp.int32)
    return as_i32(fwd), as_i32(bwd), as_i32(reset)


def kernel(x_prompt, x_sample, g_mix, w_in, b_gates, g_mlstm_out, g_na_out, rpb, w_out, g_ffn, w_router,
           b_router, w_gate_up, b_gate_up, w_down, b_down, g_final):
    assert g_mix.shape[0] == 1, "single layer"
    bp, tp, d = x_prompt.shape
    bs, ts, _ = x_sample.shape
    np_, ns = bp * tp, bs * ts
    n = np_ + ns
    H = b_gates.shape[1] // 4
    na_heads = rpb.shape[1]
    wm = g_mlstm_out.shape[1]
    wa = g_na_out.shape[1]
    dv = wm // H
    dk = dv // 2
    da = wa // na_heads
    E = w_router.shape[2]
    F = w_down.shape[2]
    L = min(MLSTM_CHUNK, tp, ts)
    assert H <= 8 and dk == LANES and da == LANES and E <= LANES and np_ % ts == 0

    xp = x_prompt.reshape(np_, d)
    xs = x_sample.reshape(ns, d)

    w = w_in[0]
    o0 = 2 * H * dk + 2 * wm
    w_main = jnp.concatenate([w[:, :o0], w[:, o0 + 4 * H:]], axis=1).astype(BF16)
    gate_tabs = _gate_weight_tables(w[:, o0:o0 + 4 * H], b_gates[0], H)
    q_off, k_off, v_off, om_off = 0, H * dk, 2 * H * dk, 2 * H * dk + wm
    qa_off, ka_off, va_off = o0, o0 + wa, o0 + 2 * wa

    h = _norm_in(xp, xs, g_mix)
    proj = _matmul(h, w_main)
    col, row = _gates(h, *gate_tabs, L)
    sched = _mlstm_schedule([tp] * bp + [ts] * bs, L)
    hf, hb = _mlstm(proj, col, row, sched, H, L, dk, dv, q_off, k_off, v_off)
    bias = _na_bias(rpb[0])
    hap = _na(proj, bias, g_na_out, bp, 0, tp, na_heads, da, qa_off, ka_off, va_off)
    has_ = _na(proj, bias, g_na_out, bs, np_, ts, na_heads, da, qa_off, ka_off, va_off)
    x1 = _out_proj(hf, hb, proj, om_off, hap, has_, g_mlstm_out, w_out[0].astype(BF16), xp, xs, H, dv)

    wr = jnp.pad(w_router[0], ((0, 0), (0, LANES - E)))
    wr_hi = wr.astype(BF16)
    wr_lo = (wr - wr_hi.astype(F32)).astype(BF16)
    br = jnp.pad(b_router, ((0, 0), (0, LANES - E)))
    idx, gate, rank, cnt, hpk = _router(x1, g_ffn, wr_hi, wr_lo, br, E)
    idx, rank = idx[:, :TOP_K], rank[:, :TOP_K]

    tm = 512 if (n * TOP_K) % 512 == 0 and n * TOP_K >= 512 * E else 256
    tf = min(512, F)
    nb = n * TOP_K // tm + E
    counts = cnt[0, :E].astype(jnp.int32)
    nblk = (counts + tm - 1) // tm
    pend = jnp.cumsum(nblk) * tm
    pstart = pend - nblk * tm
    dest = (pstart[idx] + rank).reshape(-1)
    slots = jnp.full(((nb + 1) * tm,), -1, jnp.int32).at[dest].set(jnp.arange(n * TOP_K, dtype=jnp.int32),
                                                                 unique_indices=True).reshape(nb + 1, tm)
    real = slots >= 0
    src_row = jnp.where(real, slots >> 2, 0)
    dst_row = jnp.where(real, (slots & (TOP_K - 1)) * n + (slots >> 2),
                        n * TOP_K + jnp.arange(tm, dtype=jnp.int32)[None, :])
    rowtab = jnp.concatenate([src_row, dst_row], axis=1)[:, None, :]
    blk0 = jnp.arange(nb, dtype=jnp.int32) * tm
    block_expert = jnp.minimum(jnp.sum(blk0[:, None] >= pend[None, :], axis=1), E - 1).astype(jnp.int32)
    block_nvalid = jnp.clip(counts[block_expert] - (blk0 - pstart[block_expert]), 0, tm).astype(jnp.int32)
    nused = (pend[-1:] // tm).astype(jnp.int32)

    y = _experts(block_expert, block_nvalid, nused, rowtab, hpk, w_gate_up[0].astype(BF16),
                 w_down[0].astype(BF16), b_gate_up[0][:, None, :], b_down[0][:, None, :], tm, tf)

    gf = g_final[None, :]
    y_prompt = _combine(x1, gate, y, gf, 0, np_, n).reshape(bp, tp, d)
    y_sample = _combine(x1, gate, y, gf, np_, ns, n).reshape(bs, ts, d)
    return (y_prompt, y_sample)
```

```python
import functools

import jax
import jax.numpy as jnp
from jax import lax
from jax.experimental import pallas as pl
from jax.experimental.pallas import tpu as pltpu

GRID_W = 64
GATE_SOFTCAP = 15.0
NA_WIN_ROWS = 8
NA_WIN_COLS = 16
TOP_K = 4
SWIGLU_LIMIT = 7.0
SWIGLU_ALPHA = 1.702
RMS_EPS = 1e-6

LANES = 128
SUBLANES = 8
MLSTM_CHUNK = 256
NA_ROWS_PER_STEP = 16
NEG_BIG = -1e30
MIB = 1024 * 1024

F32 = jnp.float32
BF16 = jnp.bfloat16


def _params(vmem_mib, n_axes):
    return pltpu.CompilerParams(
        dimension_semantics=("arbitrary",) * n_axes, vmem_limit_bytes=vmem_mib * MIB)


def _dot(a, b):
    return jnp.dot(a, b, preferred_element_type=F32)


def _dot_nt(a, b):
    return lax.dot_general(a, b, (((1,), (1,)), ((), ())), preferred_element_type=F32)


def _dot_tn(a, b):
    return lax.dot_general(a, b, (((0,), (0,)), ((), ())), preferred_element_type=F32)


def _split3(x):
    hi = x.astype(BF16)
    r1 = x - hi.astype(F32)
    mid = r1.astype(BF16)
    lo = (r1 - mid.astype(F32)).astype(BF16)
    return hi, mid, lo


def _rms(x, g):
    return x * lax.rsqrt(jnp.mean(x * x, axis=-1, keepdims=True) + RMS_EPS) * g


def _norm_in_kernel(nbp, xp_ref, xs_ref, g_ref, o_ref):
    i = pl.program_id(0)
    x = jnp.where(i < nbp, xp_ref[...], xs_ref[...])
    o_ref[...] = _rms(x, g_ref[...]).astype(o_ref.dtype)


def _norm_in(xp, xs, g, tm=256):
    np_, d = xp.shape
    ns = xs.shape[0]
    nbp, nbs = np_ // tm, ns // tm
    return pl.pallas_call(
        functools.partial(_norm_in_kernel, nbp),
        out_shape=jax.ShapeDtypeStruct((np_ + ns, d), BF16),
        grid=(nbp + nbs,),
        in_specs=[
            pl.BlockSpec((tm, d), lambda i: (jnp.minimum(i, nbp - 1), 0)),
            pl.BlockSpec((tm, d), lambda i: (jnp.maximum(i - nbp, 0), 0)),
            pl.BlockSpec((1, d), lambda i: (0, 0)),
        ],
        out_specs=pl.BlockSpec((tm, d), lambda i: (i, 0)),
        compiler_params=_params(48, 1),
        name="norm_in",
    )(xp, xs, g)


def _matmul_kernel(a_ref, b_ref, o_ref):
    o_ref[...] = _dot(a_ref[...], b_ref[...]).astype(o_ref.dtype)


def _matmul(a, b, tm=1024, tn=1024):
    m, k = a.shape
    n = b.shape[1]
    tm, tn = min(tm, m), min(tn, n)
    return pl.pallas_call(
        _matmul_kernel,
        out_shape=jax.ShapeDtypeStruct((m, n), BF16),
        grid=(m // tm, n // tn),
        in_specs=[pl.BlockSpec((tm, k), lambda i, j: (i, 0)),
                  pl.BlockSpec((k, tn), lambda i, j: (0, j))],
        out_specs=pl.BlockSpec((tm, tn), lambda i, j: (i, j)),
        compiler_params=_params(48, 2),
        name="in_proj",
    )(a, b)


def _softcap(x):
    return GATE_SOFTCAP * jnp.tanh(x / GATE_SOFTCAP)


def _log_sigmoid(x):
    return jnp.minimum(x, 0.0) - jnp.log1p(jnp.exp(-jnp.abs(x)))


def _gates_kernel(L, h_ref, wi_ref, wf_ref, bi_ref, bf_ref, wit_ref, wft_ref, bit_ref, bft_ref,
                  col_ref, row_ref):
    h = h_ref[...]
    t = h.shape[0]
    gi = _softcap(_dot(h, wi_ref[...]) + bi_ref[...])
    lf = _log_sigmoid(_softcap(_dot(h, wf_ref[...]) + bf_ref[...]))
    git = _softcap(_dot_nt(wit_ref[...], h) + bit_ref[...])
    lft = _log_sigmoid(_softcap(_dot_nt(wft_ref[...], h) + bft_ref[...]))

    r = lax.broadcasted_iota(jnp.int32, (L, L), 0)
    c = lax.broadcasted_iota(jnp.int32, (L, L), 1)
    tri = (c <= r).astype(BF16)
    trit = (r <= c).astype(BF16)

    lane = lax.broadcasted_iota(jnp.int32, (L, LANES), 1)
    q = lane % 16
    is_bwd = q >= 3
    kind = jnp.where(is_bwd, q - 3, q)
    rowi = lax.broadcasted_iota(jnp.int32, (16, L), 0)
    row_bwd = (rowi % 2) == 1

    for ci in range(t // L):
        sl = slice(ci * L, (ci + 1) * L)
        lfc, gic = lf[sl], gi[sl]
        hi, mid, lo = _split3(lfc)
        b = _dot(tri, hi) + _dot(tri, mid) + _dot(tri, lo)
        a = b - lfc
        tot = b[L - 1:L, :]
        alpha = jnp.where(is_bwd, -a, b)
        gamma = jnp.where(is_bwd, tot - a, b)
        omega = jnp.where(is_bwd, a + gic, tot - b + gic)
        col_ref[sl, :] = jnp.where(kind == 0, alpha, jnp.where(kind == 1, gamma, omega))

        lfct, gict = lft[:, sl], git[:, sl]
        hi, mid, lo = _split3(lfct)
        bt = _dot(hi, trit) + _dot(mid, trit) + _dot(lo, trit)
        at = bt - lfct
        row_ref[:, sl] = jnp.where(row_bwd, at + gict, gict - bt)


def _gates(h, wi, wf, bi, bf, wit, wft, bit, bft, L, tm=1024):
    n, d = h.shape
    tm = min(tm, n)
    full = lambda i: (0, 0)
    return pl.pallas_call(
        functools.partial(_gates_kernel, L),
        out_shape=(jax.ShapeDtypeStruct((n, LANES), F32), jax.ShapeDtypeStruct((16, n), F32)),
        grid=(n // tm,),
        in_specs=[pl.BlockSpec((tm, d), lambda i: (i, 0)),
                  pl.BlockSpec((d, LANES), full), pl.BlockSpec((d, LANES), full),
                  pl.BlockSpec((1, LANES), full), pl.BlockSpec((1, LANES), full),
                  pl.BlockSpec((16, d), full), pl.BlockSpec((16, d), full),
                  pl.BlockSpec((16, 1), full), pl.BlockSpec((16, 1), full)],
        out_specs=(pl.BlockSpec((tm, LANES), lambda i: (i, 0)),
                   pl.BlockSpec((16, tm), lambda i: (0, i))),
        compiler_params=_params(40, 1),
        name="mlstm_gates",
    )(h, wi, wf, bi, bf, wit, wft, bit, bft)


def _mlstm_kernel(H, L, dk, dv, fwd_ref, bwd_ref, reset_ref,
                  qf_ref, kf_ref, vf_ref, colf_ref, rowf_ref,
                  qb_ref, kb_ref, vb_ref, colb_ref, rowb_ref,
                  hf_ref, hb_ref, ct_ref, nr_ref):
    s = pl.program_id(0)

    @pl.when(reset_ref[s] == 1)
    def _():
        ct_ref[...] = jnp.zeros_like(ct_ref)
        nr_ref[...] = jnp.zeros_like(nr_ref)

    r = lax.broadcasted_iota(jnp.int32, (L, L), 0)
    c = lax.broadcasted_iota(jnp.int32, (L, L), 1)
    masks = (c <= r, c >= r)
    ones = jnp.ones((L, LANES), BF16)
    scale = dk ** -0.5
    dirs = ((qf_ref, kf_ref, vf_ref, colf_ref, rowf_ref, hf_ref),
            (qb_ref, kb_ref, vb_ref, colb_ref, rowb_ref, hb_ref))

    for hd in range(H):
        for d, (q_ref, k_ref, v_ref, col_ref, row_ref, o_ref) in enumerate(dirs):
            ch = hd * 2 + d
            q = q_ref[:, hd * dk:(hd + 1) * dk]
            k = k_ref[:, hd * dk:(hd + 1) * dk]
            v = v_ref[:, hd * dv:(hd + 1) * dv]
            lane0 = hd * 16 + d * 3
            alpha = col_ref[:, lane0:lane0 + 1]
            gamma = col_ref[:, lane0 + 1:lane0 + 2]
            omega = col_ref[:, lane0 + 2:lane0 + 3]
            beta = row_ref[ch:ch + 1, :]
            g_end = gamma[L - 1:L, :] if d == 0 else gamma[0:1, :]

            logd = jnp.where(masks[d], alpha + beta, -jnp.inf)
            p = (_dot_nt(q, k) * scale) * jnp.exp(logd)
            pb = p.astype(BF16)
            ct = ct_ref[ch]
            nr = nr_ref[ch]
            eg = jnp.exp(gamma) * scale
            num = _dot(pb, v) + eg * _dot(q, ct.astype(BF16))
            den = _dot(pb, ones) + eg * _dot(q, nr.astype(BF16))
            inv = 1.0 / jnp.maximum(jnp.abs(den), 1.0)
            o_ref[:, hd * dv:(hd + 1) * dv] = (
                num * jnp.concatenate([inv] * (dv // LANES), axis=1)).astype(o_ref.dtype)

            kw = (k.astype(F32) * jnp.exp(omega)).astype(BF16)
            decay = jnp.exp(g_end)
            ct_ref[ch] = decay * ct + _dot_tn(kw, v)
            nr_ref[ch] = decay * nr + _dot_tn(kw, ones)


def _mlstm(proj, col, row, sched, H, L, dk, dv, q_off, k_off, v_off):
    n = proj.shape[0]
    fwd, bwd, reset = sched
    steps = fwd.shape[0]
    qw, vw = H * dk, H * dv
    qb, kb, vb = q_off // qw, k_off // qw, v_off // vw
    fi = lambda s, f, b, r: f[s]
    bi = lambda s, f, b, r: b[s]

    def specs(pick):
        return [pl.BlockSpec((L, qw), lambda s, f, b, r: (pick(s, f, b, r), qb)),
                pl.BlockSpec((L, qw), lambda s, f, b, r: (pick(s, f, b, r), kb)),
                pl.BlockSpec((L, vw), lambda s, f, b, r: (pick(s, f, b, r), vb)),
                pl.BlockSpec((L, LANES), lambda s, f, b, r: (pick(s, f, b, r), 0)),
                pl.BlockSpec((16, L), lambda s, f, b, r: (0, pick(s, f, b, r)))]

    grid_spec = pltpu.PrefetchScalarGridSpec(
        num_scalar_prefetch=3, grid=(steps,),
        in_specs=specs(fi) + specs(bi),
        out_specs=(pl.BlockSpec((L, vw), lambda s, f, b, r: (f[s], 0)),
                   pl.BlockSpec((L, vw), lambda s, f, b, r: (b[s], 0))),
        scratch_shapes=[pltpu.VMEM((2 * H, dk, dv), F32), pltpu.VMEM((2 * H, dk, LANES), F32)])
    return pl.pallas_call(
        functools.partial(_mlstm_kernel, H, L, dk, dv),
        out_shape=(jax.ShapeDtypeStruct((n, vw), BF16), jax.ShapeDtypeStruct((n, vw), BF16)),
        grid_spec=grid_spec,
        compiler_params=_params(40, 1),
        name="mlstm",
    )(fwd, bwd, reset, proj, proj, proj, col, row, proj, proj, proj, col, row)


def _na_kernel(rows, d, q_ref, k_ref, v_ref, bias_ref, g_ref, o_ref):
    scale = d ** -0.5
    win = NA_WIN_ROWS * GRID_W
    nq = NA_ROWS_PER_STEP * GRID_W
    g = g_ref[...]
    ones = jnp.ones((win, LANES), BF16)

    def block(rb, carry):
        starts, logits = [], []
        for u in range(NA_ROWS_PER_STEP):
            r = rb * NA_ROWS_PER_STEP + u
            rs = jnp.clip(r - NA_WIN_ROWS // 2, 0, rows - NA_WIN_ROWS)
            ks = pl.multiple_of(rs * GRID_W, GRID_W)
            q = q_ref[pl.ds(pl.multiple_of(r * GRID_W, GRID_W), GRID_W), :]
            logits.append(_dot_nt(q, k_ref[pl.ds(ks, win), :]) * scale + bias_ref[r - rs])
            starts.append(ks)
        s = jnp.concatenate(logits, axis=0)
        pb = jnp.exp(s - jnp.max(s, axis=-1, keepdims=True)).astype(BF16)
        l = _dot(pb, ones)
        o = jnp.concatenate(
            [_dot(pb[u * GRID_W:(u + 1) * GRID_W], v_ref[pl.ds(starts[u], win), :])
             for u in range(NA_ROWS_PER_STEP)], axis=0) / l
        o_ref[pl.ds(pl.multiple_of(rb * nq, nq), nq), :] = _rms(o, g).astype(o_ref.dtype)
        return carry

    lax.fori_loop(0, rows // NA_ROWS_PER_STEP, block, 0)


def _na(proj, bias, g, nseq, row0, T, heads, d, q_off, k_off, v_off):
    rows = T // GRID_W
    sb = row0 // T
    qb, kb, vb = q_off // d, k_off // d, v_off // d
    return pl.pallas_call(
        functools.partial(_na_kernel, rows, d),
        out_shape=jax.ShapeDtypeStruct((nseq * T, heads * d), BF16),
        grid=(nseq, heads),
        in_specs=[pl.BlockSpec((T, d), lambda b, h: (sb + b, qb + h)),
                  pl.BlockSpec((T, d), lambda b, h: (sb + b, kb + h)),
                  pl.BlockSpec((T, d), lambda b, h: (sb + b, vb + h)),
                  pl.BlockSpec((None, NA_WIN_ROWS, GRID_W, NA_WIN_ROWS * GRID_W), lambda b, h: (h, 0, 0, 0)),
                  pl.BlockSpec((1, d), lambda b, h: (0, h))],
        out_specs=pl.BlockSpec((T, d), lambda b, h: (b, h)),
        compiler_params=_params(48, 2),
        name="natten",
    )(proj, proj, proj, bias, g)


def _na_bias(rpb):
    heads = rpb.shape[0]
    ncol = 2 * NA_WIN_COLS - 1
    cq = jnp.arange(GRID_W)
    cs = jnp.clip(cq - NA_WIN_COLS // 2, 0, GRID_W - NA_WIN_COLS)
    j = jnp.arange(GRID_W)
    inside = (j[None, :] >= cs[:, None]) & (j[None, :] < cs[:, None] + NA_WIN_COLS)
    coff = j[None, :] - cq[:, None] + (NA_WIN_COLS - 1)
    pick = ((coff[None] == jnp.arange(ncol)[:, None, None]) & inside[None]).astype(F32)
    rsel = jnp.stack([rpb.astype(F32)[:, NA_WIN_ROWS - 1 - var:2 * NA_WIN_ROWS - 1 - var, :]
                      for var in range(NA_WIN_ROWS)], axis=1)
    tab = jnp.einsum('hvic,cqj->hvqij', rsel, pick, precision=lax.Precision.HIGHEST)
    tab = jnp.where(inside[None, None, :, None, :], tab, NEG_BIG)
    return tab.reshape(heads, NA_WIN_ROWS, GRID_W, NA_WIN_ROWS * GRID_W)


def _mlstm_out_kernel(H, dv, hf_ref, hb_ref, om_ref, g_ref, o_ref):
    for hd in range(H):
        sl = slice(hd * dv, (hd + 1) * dv)
        hs = hf_ref[:, sl].astype(F32) + hb_ref[:, sl].astype(F32)
        y = _rms(hs, g_ref[:, sl]) * jax.nn.sigmoid(om_ref[:, sl].astype(F32))
        o_ref[:, sl] = y.astype(o_ref.dtype)


def _mlstm_out(hf, hb, proj, om_off, g_ml, H, dv, tm=512):
    n, wm = hf.shape
    tm = min(tm, n)
    omb = om_off // wm
    row = lambda i: (i, 0)
    return pl.pallas_call(
        functools.partial(_mlstm_out_kernel, H, dv),
        out_shape=jax.ShapeDtypeStruct((n, wm), BF16),
        grid=(n // tm,),
        in_specs=[pl.BlockSpec((tm, wm), row), pl.BlockSpec((tm, wm), row),
                  pl.BlockSpec((tm, wm), lambda i: (i, omb)), pl.BlockSpec((1, wm), lambda i: (0, 0))],
        out_specs=pl.BlockSpec((tm, wm), row),
        compiler_params=_params(40, 1),
        name="mlstm_out",
    )(hf, hb, proj, g_ml)


def _out_proj_kernel(nbp, wm, hm_ref, hap_ref, has_ref, w_ref, xp_ref, xs_ref, o_ref):
    i = pl.program_id(0)
    ha = jnp.where(i < nbp, hap_ref[...], has_ref[...])
    x = jnp.where(i < nbp, xp_ref[...], xs_ref[...])
    o_ref[...] = x + _dot(hm_ref[...], w_ref[:wm, :]) + _dot(ha, w_ref[wm:, :])


def _out_proj(hm, hap, has_, w_out, xp, xs, tm=1024, tn=512):
    n, wm = hm.shape
    wa = hap.shape[1]
    d = w_out.shape[1]
    tm, tn = min(tm, xp.shape[0], xs.shape[0]), min(tn, d)
    nbp = xp.shape[0] // tm
    pidx = lambda i, j: (jnp.minimum(i, nbp - 1), 0)
    sidx = lambda i, j: (jnp.maximum(i - nbp, 0), 0)
    return pl.pallas_call(
        functools.partial(_out_proj_kernel, nbp, wm),
        out_shape=jax.ShapeDtypeStruct((n, d), F32),
        grid=(n // tm, d // tn),
        in_specs=[pl.BlockSpec((tm, wm), lambda i, j: (i, 0)),
                  pl.BlockSpec((tm, wa), pidx),
                  pl.BlockSpec((tm, wa), sidx),
                  pl.BlockSpec((wm + wa, tn), lambda i, j: (0, j)),
                  pl.BlockSpec((tm, tn), lambda i, j: (jnp.minimum(i, nbp - 1), j)),
                  pl.BlockSpec((tm, tn), lambda i, j: (jnp.maximum(i - nbp, 0), j))],
        out_specs=pl.BlockSpec((tm, tn), lambda i, j: (i, j)),
        compiler_params=_params(52, 2),
        name="out_proj",
    )(hm, hap, has_, w_out, xp, xs)


def _pack_bf16_pair(lo, hi):
    lo_bits = lax.bitcast_convert_type(lo.astype(BF16).astype(F32), jnp.uint32)
    hi_bits = lax.bitcast_convert_type(hi.astype(BF16).astype(F32), jnp.uint32)
    return hi_bits | (lo_bits >> 16)


def _unpack_bf16_pair(w):
    lo = lax.bitcast_convert_type(w << 16, F32)
    hi = lax.bitcast_convert_type(w & jnp.uint32(0xFFFF0000), F32)
    return lo, hi


def _router_kernel(E, x_ref, g_ref, whi_ref, wlo_ref, b_ref, idx_ref, gate_ref, rank_ref, cnt_ref, hpk_ref,
                   carry_ref):
    i = pl.program_id(0)

    @pl.when(i == 0)
    def _():
        carry_ref[...] = jnp.zeros_like(carry_ref)

    h = _rms(x_ref[...], g_ref[...])
    tm, d = h.shape
    hpk_ref[...] = _pack_bf16_pair(h[:, :d // 2], h[:, d // 2:])
    hhi = h.astype(BF16)
    hlo = (h - hhi.astype(F32)).astype(BF16)
    whi = whi_ref[...]
    logits = _dot(hhi, whi) + _dot(hlo, whi) + _dot(hhi, wlo_ref[...]) + b_ref[...]

    lane = lax.broadcasted_iota(jnp.int32, (tm, LANES), 1)
    work = jnp.where(lane < E, logits, NEG_BIG)
    vals, idxs = [], []
    for _ in range(TOP_K):
        m = jnp.max(work, axis=-1, keepdims=True)
        am = jnp.min(jnp.where(work == m, lane, LANES), axis=-1, keepdims=True)
        vals.append(m)
        idxs.append(am)
        work = jnp.where(lane == am, -jnp.inf, work)
    es = [jnp.exp(v - vals[0]) for v in vals]
    denom = es[0] + es[1] + es[2] + es[3]

    sel = (work == -jnp.inf).astype(BF16)
    r = lax.broadcasted_iota(jnp.int32, (tm, tm), 0)
    c = lax.broadcasted_iota(jnp.int32, (tm, tm), 1)
    before = _dot((c < r).astype(BF16), sel) + carry_ref[0:1, :]
    carry_ref[...] = carry_ref[...] + jnp.sum(sel.astype(F32), axis=0, keepdims=True)
    cnt_ref[...] = carry_ref[...]

    idx_o = jnp.zeros((tm, LANES), jnp.int32)
    gate_o = jnp.zeros((tm, LANES), F32)
    rank_o = jnp.zeros((tm, LANES), jnp.int32)
    for k in range(TOP_K):
        rk = jnp.sum(jnp.where(lane == idxs[k], before, 0.0), axis=-1, keepdims=True)
        idx_o = jnp.where(lane == k, idxs[k], idx_o)
        gate_o = jnp.where(lane == k, es[k] / denom, gate_o)
        rank_o = jnp.where(lane == k, rk.astype(jnp.int32), rank_o)
    idx_ref[...] = idx_o
    gate_ref[...] = gate_o
    rank_ref[...] = rank_o


def _router(x1, g, whi, wlo, b, E, tm=256):
    n, d = x1.shape
    tm = min(tm, n)
    full = lambda i: (0, 0)
    tok = pl.BlockSpec((tm, LANES), lambda i: (i, 0))
    return pl.pallas_call(
        functools.partial(_router_kernel, E),
        out_shape=(jax.ShapeDtypeStruct((n, LANES), jnp.int32), jax.ShapeDtypeStruct((n, LANES), F32),
                   jax.ShapeDtypeStruct((n, LANES), jnp.int32), jax.ShapeDtypeStruct((8, LANES), F32),
                   jax.ShapeDtypeStruct((n, d // 2), jnp.uint32)),
        grid=(n // tm,),
        in_specs=[pl.BlockSpec((tm, d), lambda i: (i, 0)), pl.BlockSpec((1, d), full),
                  pl.BlockSpec((d, LANES), full), pl.BlockSpec((d, LANES), full),
                  pl.BlockSpec((1, LANES), full)],
        out_specs=(tok, tok, tok, pl.BlockSpec((8, LANES), full), pl.BlockSpec((tm, d // 2), lambda i: (i, 0))),
        scratch_shapes=[pltpu.VMEM((8, LANES), F32)],
        compiler_params=_params(40, 1),
        name="router",
    )(x1, g, whi, wlo, b)


def _expert_kernel(tm, n_tok, nf_static, be_ref, nv_ref, nused_ref, slots_hbm, hpk_hbm, wg_ref, wu_ref, wd_ref,
                   bg_ref, bu_ref, bd_ref, y_hbm, idx_smem, gbuf_ref, xb_ref, acc_ref, ybuf_ref,
                   sem_idx, sem_in, sem_out):
    i = pl.program_id(0)
    f = pl.program_id(1)
    nf = nf_static
    nused = nused_ref[0]
    dh = gbuf_ref.shape[2]
    ngrp = tm // SUBLANES
    spare_row = slots_hbm.shape[0] - 1


    def idx_base(ring):
        return pl.multiple_of(ring * (2 * tm), 2 * tm)

    def idx_fetch(row, ring):
        return pltpu.make_async_copy(slots_hbm.at[row, 0], idx_smem.at[pl.ds(idx_base(ring), 2 * tm)], sem_idx)

    def in_row(tok, gi, u):
        return pltpu.make_async_copy(hpk_hbm.at[pl.ds(tok, 1), :], gbuf_ref.at[gi, pl.ds(u, 1), :], sem_in)

    def gather_start(ring):
        base = idx_base(ring)

        def group(gi, carry):
            for u in range(SUBLANES):
                in_row(idx_smem[base + gi * SUBLANES + u], gi, u).start()
            return carry

        lax.fori_loop(0, ngrp, group, 0)

    def gather_wait():
        def group(gi, carry):
            pltpu.make_async_copy(hpk_hbm.at[pl.ds(0, SUBLANES), :], gbuf_ref.at[0], sem_in).wait()
            return carry

        lax.fori_loop(0, ngrp, group, 0)

    def out_row(gi, u, dst):
        return pltpu.make_async_copy(ybuf_ref.at[gi, pl.ds(u, 1), :], y_hbm.at[pl.ds(dst, 1), :], sem_out)

    def out_group():
        return pltpu.make_async_copy(ybuf_ref.at[0], y_hbm.at[pl.ds(0, SUBLANES), :], sem_out)

    def scatter_wait_all():
        def group(gi, carry):
            out_group().wait()
            return carry

        lax.fori_loop(0, ngrp, group, 0)

    def scatter_start(ring, nv):
        base = idx_base(ring) + tm

        def dst_row(s):
            return idx_smem[base + s]

        def group(gi, carry):
            for u in range(SUBLANES):
                out_row(gi, u, dst_row(gi * SUBLANES + u)).start()
            return carry

        def single(s, carry):
            out_row(lax.shift_right_logical(s, 3), s & (SUBLANES - 1), dst_row(s)).start()
            return carry

        n8 = lax.shift_right_logical(nv, 3)
        lax.fori_loop(0, n8, group, 0)
        lax.fori_loop(n8 * SUBLANES, nv, single, 0)

    def scatter_wait(nv):
        def group(gi, carry):
            out_group().wait()
            return carry

        def single(s, carry):
            out_row(0, 0, 0).wait()
            return carry

        n8 = lax.shift_right_logical(nv, 3)
        lax.fori_loop(0, n8, group, 0)
        lax.fori_loop(n8 * SUBLANES, nv, single, 0)

    @pl.when(i < nused)
    def _():
        nv = nv_ref[i]
        ring_cur = lax.rem(i, 3)
        ring_next = lax.rem(i + 1, 3)
        ring_prev = lax.rem(i + 2, 3)

        @pl.when(f == 0)
        def _():
            @pl.when(i == 0)
            def _():
                for row, ring in ((0, 0), (spare_row, 2)):
                    cp = idx_fetch(row, ring)
                    cp.start()
                    cp.wait()
                gather_start(0)
                ybuf_ref[...] = jnp.zeros_like(ybuf_ref)

            idx_fetch(i + 1, ring_next).start()
            gather_wait()
            lo, hi = _unpack_bf16_pair(gbuf_ref[...].reshape(tm, dh))
            xb_ref[:, :dh] = lo.astype(BF16)
            xb_ref[:, dh:] = hi.astype(BF16)
            acc_ref[...] = jnp.broadcast_to(bd_ref[0], acc_ref.shape)
            idx_fetch(i + 1, ring_next).wait()

        def neighbour_dma_part():
            gbase = idx_base(ring_next)
            sbase = idx_base(ring_prev) + tm
            gps = ngrp // nf
            for j in range(gps):
                gi = f * gps + j
                for u in range(SUBLANES):
                    s = gi * SUBLANES + u
                    in_row(idx_smem[gbase + s], gi, u).start()
                    out_row(gi, u, idx_smem[sbase + s]).start()

        def ffn(rows):
            neighbour_dma_part()
            x = xb_ref[:rows, :]
            gg = jnp.minimum(_dot(x, wg_ref[0]) + bg_ref[0], SWIGLU_LIMIT)
            uu = jnp.clip(_dot(x, wu_ref[0]) + bu_ref[0], -SWIGLU_LIMIT, SWIGLU_LIMIT)
            act = ((uu + 1.0) * (gg * jax.nn.sigmoid(gg * SWIGLU_ALPHA))).astype(BF16)
            acc_ref[:rows, :] += _dot(act, wd_ref[0])

        @pl.when(nv > tm // 2)
        def _():
            ffn(tm)

        @pl.when(nv <= tm // 2)
        def _():
            ffn(tm // 2)

        @pl.when(f == nf - 1)
        def _():
            scatter_wait_all()
            y = acc_ref[...]
            ybuf_ref[...] = _pack_bf16_pair(y[:, :dh], y[:, dh:]).reshape(ngrp, SUBLANES, dh)

            @pl.when(i == nused - 1)
            def _():
                scatter_start(ring_cur, nv)
                scatter_wait(nv)
                gather_wait()


def _experts(block_expert, block_nvalid, nused, slots, hpk, wgu, wd, bgu, bd, tm, tf):
    n, dh = hpk.shape
    d = 2 * dh
    E, _, f2 = wgu.shape
    F = f2 // 2
    nb = slots.shape[0] - 1
    nf = F // tf
    assert (tm // SUBLANES) % nf == 0

    def fidx(i, f, nu):
        return jnp.where(i < nu[0], f, nf - 1)

    def bidx(i, be, nu):
        return be[jnp.minimum(i, nu[0] - 1)]

    grid_spec = pltpu.PrefetchScalarGridSpec(
        num_scalar_prefetch=3, grid=(nb, nf),
        in_specs=[pl.BlockSpec(memory_space=pl.ANY), pl.BlockSpec(memory_space=pl.ANY),
                  pl.BlockSpec((1, d, tf), lambda i, f, be, nv, nu: (bidx(i, be, nu), 0, fidx(i, f, nu))),
                  pl.BlockSpec((1, d, tf), lambda i, f, be, nv, nu: (bidx(i, be, nu), 0, nf + fidx(i, f, nu))),
                  pl.BlockSpec((1, tf, d), lambda i, f, be, nv, nu: (bidx(i, be, nu), fidx(i, f, nu), 0)),
                  pl.BlockSpec((1, 1, tf), lambda i, f, be, nv, nu: (bidx(i, be, nu), 0, fidx(i, f, nu))),
                  pl.BlockSpec((1, 1, tf), lambda i, f, be, nv, nu: (bidx(i, be, nu), 0, nf + fidx(i, f, nu))),
                  pl.BlockSpec((1, 1, d), lambda i, f, be, nv, nu: (bidx(i, be, nu), 0, 0))],
        out_specs=pl.BlockSpec(memory_space=pl.ANY),
        scratch_shapes=[pltpu.SMEM((3 * 2 * tm,), jnp.int32), pltpu.VMEM((tm // SUBLANES, SUBLANES, dh), jnp.uint32),
                        pltpu.VMEM((tm, d), BF16), pltpu.VMEM((tm, d), F32), pltpu.VMEM((tm // SUBLANES, SUBLANES, dh), jnp.uint32),
                        pltpu.SemaphoreType.DMA, pltpu.SemaphoreType.DMA, pltpu.SemaphoreType.DMA])
    return pl.pallas_call(
        functools.partial(_expert_kernel, tm, n, nf),
        out_shape=jax.ShapeDtypeStruct((n * TOP_K + tm, dh), jnp.uint32),
        grid_spec=grid_spec,
        compiler_params=_params(56, 2),
        name="experts",
    )(block_expert, block_nvalid, nused, slots, hpk, wgu, wgu, wd, bgu, bgu, bd)


def _combine_kernel(x_ref, gate_ref, y0_ref, y1_ref, y2_ref, y3_ref, g_ref, o_ref):
    d = x_ref.shape[1]
    dh = d // 2
    lo_acc, hi_acc = x_ref[:, :dh], x_ref[:, dh:]
    for k, y_ref in enumerate((y0_ref, y1_ref, y2_ref, y3_ref)):
        lo, hi = _unpack_bf16_pair(y_ref[...])
        gk = gate_ref[:, k:k + 1]
        lo_acc = lo_acc + gk * lo
        hi_acc = hi_acc + gk * hi
    ms = (jnp.sum(lo_acc * lo_acc, axis=-1, keepdims=True) + jnp.sum(hi_acc * hi_acc, axis=-1, keepdims=True)) / d
    r = lax.rsqrt(ms + RMS_EPS)
    o_ref[:, :dh] = lo_acc * r * g_ref[:, :dh]
    o_ref[:, dh:] = hi_acc * r * g_ref[:, dh:]


def _combine(x1, gate, y, g, row0, rows, n_tok, tm=256):
    d = x1.shape[1]
    tm = min(tm, rows)
    b0 = row0 // tm
    nbt = n_tok // tm
    ys = [pl.BlockSpec((tm, d // 2), functools.partial(lambda i, k: (k * nbt + b0 + i, 0), k=k))
          for k in range(TOP_K)]
    return pl.pallas_call(
        _combine_kernel,
        out_shape=jax.ShapeDtypeStruct((rows, d), F32),
        grid=(rows // tm,),
        in_specs=[pl.BlockSpec((tm, d), lambda i: (b0 + i, 0)),
                  pl.BlockSpec((tm, LANES), lambda i: (b0 + i, 0))] + ys +
                 [pl.BlockSpec((1, d), lambda i: (0, 0))],
        out_specs=pl.BlockSpec((tm, d), lambda i: (i, 0)),
        compiler_params=_params(48, 1),
        name="combine",
    )(x1, gate, y, y, y, y, g)


def _gate_weight_tables(w_g, b_g, H):
    d = w_g.shape[0]
    wi = jnp.stack([w_g[:, 0:H], w_g[:, 2 * H:3 * H]], axis=-1)
    wf = jnp.stack([w_g[:, H:2 * H], w_g[:, 3 * H:4 * H]], axis=-1)
    bi = jnp.stack([b_g[0:H], b_g[2 * H:3 * H]], axis=-1)
    bf = jnp.stack([b_g[H:2 * H], b_g[3 * H:4 * H]], axis=-1)

    def col(w):
        w6 = jnp.repeat(w, 3, axis=-1)
        w16 = jnp.pad(w6, [(0, 0)] * (w6.ndim - 1) + [(0, 10)]).reshape(w.shape[:-2] + (H * 16,))
        return jnp.pad(w16, [(0, 0)] * (w16.ndim - 1) + [(0, LANES - H * 16)])

    def rowt(w):
        return jnp.pad(w, ((0, 16 - 2 * H), (0, 0)))

    wit = rowt(wi.reshape(d, 2 * H).T)
    wft = rowt(wf.reshape(d, 2 * H).T)
    return (col(wi).astype(BF16), col(wf).astype(BF16), col(bi)[None].astype(F32), col(bf)[None].astype(F32),
            wit.astype(BF16), wft.astype(BF16), rowt(bi.reshape(2 * H, 1)).astype(F32),
            rowt(bf.reshape(2 * H, 1)).astype(F32))


def _mlstm_schedule(seq_lens, L):
    fwd, bwd, reset, base = [], [], [], 0
    for t in seq_lens:
        nc = t // L
        for c in range(nc):
            fwd.append(base + c)
            bwd.append(base + nc - 1 - c)
            reset.append(1 if c == 0 else 0)
        base += nc
    as_i32 = lambda v: jnp.asarray(v, dtype=jnp.int32)
    return as_i32(fwd), as_i32(bwd), as_i32(reset)


def kernel(x_prompt, x_sample, g_mix, w_in, b_gates, g_mlstm_out, g_na_out, rpb, w_out, g_ffn, w_router,
           b_router, w_gate_up, b_gate_up, w_down, b_down, g_final):
    assert g_mix.shape[0] == 1, "single layer"
    bp, tp, d = x_prompt.shape
    bs, ts, _ = x_sample.shape
    np_, ns = bp * tp, bs * ts
    n = np_ + ns
    H = b_gates.shape[1] // 4
    na_heads = rpb.shape[1]
    wm = g_mlstm_out.shape[1]
    wa = g_na_out.shape[1]
    dv = wm // H
    dk = dv // 2
    da = wa // na_heads
    E = w_router.shape[2]
    F = w_down.shape[2]
    L = min(MLSTM_CHUNK, tp, ts)
    assert H <= 8 and dk == LANES and da == LANES and E <= LANES and np_ % ts == 0

    xp = x_prompt.reshape(np_, d)
    xs = x_sample.reshape(ns, d)

    w = w_in[0]
    o0 = 2 * H * dk + 2 * wm
    w_main = jnp.concatenate([w[:, :o0], w[:, o0 + 4 * H:]], axis=1).astype(BF16)
    gate_tabs = _gate_weight_tables(w[:, o0:o0 + 4 * H], b_gates[0], H)
    q_off, k_off, v_off, om_off = 0, H * dk, 2 * H * dk, 2 * H * dk + wm
    qa_off, ka_off, va_off = o0, o0 + wa, o0 + 2 * wa

    h = _norm_in(xp, xs, g_mix)
    proj = _matmul(h, w_main)
    col, row = _gates(h, *gate_tabs, L)
    sched = _mlstm_schedule([tp] * bp + [ts] * bs, L)
    hf, hb = _mlstm(proj, col, row, sched, H, L, dk, dv, q_off, k_off, v_off)
    bias = _na_bias(rpb[0])
    hap = _na(proj, bias, g_na_out, bp, 0, tp, na_heads, da, qa_off, ka_off, va_off)
    has_ = _na(proj, bias, g_na_out, bs, np_, ts, na_heads, da, qa_off, ka_off, va_off)
    hm = _mlstm_out(hf, hb, proj, om_off, g_mlstm_out, H, dv)
    x1 = _out_proj(hm, hap, has_, w_out[0].astype(BF16), xp, xs)

    wr = jnp.pad(w_router[0], ((0, 0), (0, LANES - E)))
    wr_hi = wr.astype(BF16)
    wr_lo = (wr - wr_hi.astype(F32)).astype(BF16)
    br = jnp.pad(b_router, ((0, 0), (0, LANES - E)))
    idx, gate, rank, cnt, hpk = _router(x1, g_ffn, wr_hi, wr_lo, br, E)
    idx, rank = idx[:, :TOP_K], rank[:, :TOP_K]

    tm = 512 if (n * TOP_K) % 512 == 0 and n * TOP_K >= 512 * E else 256
    tf = min(512, F)
    nb = n * TOP_K // tm + E
    counts = cnt[0, :E].astype(jnp.int32)
    nblk = (counts + tm - 1) // tm
    pend = jnp.cumsum(nblk) * tm
    pstart = pend - nblk * tm
    dest = (pstart[idx] + rank).reshape(-1)
    slots = jnp.full(((nb + 1) * tm,), -1, jnp.int32).at[dest].set(jnp.arange(n * TOP_K, dtype=jnp.int32),
                                                                 unique_indices=True).reshape(nb + 1, tm)
    real = slots >= 0
    src_row = jnp.where(real, slots >> 2, 0)
    dst_row = jnp.where(real, (slots & (TOP_K - 1)) * n + (slots >> 2),
                        n * TOP_K + jnp.arange(tm, dtype=jnp.int32)[None, :])
    rowtab = jnp.concatenate([src_row, dst_row], axis=1)[:, None, :]
    blk0 = jnp.arange(nb, dtype=jnp.int32) * tm
    block_expert = jnp.minimum(jnp.sum(blk0[:, None] >= pend[None, :], axis=1), E - 1).astype(jnp.int32)
    block_nvalid = jnp.clip(counts[block_expert] - (blk0 - pstart[block_expert]), 0, tm).astype(jnp.int32)
    nused = (pend[-1:] // tm).astype(jnp.int32)

    y = _experts(block_expert, block_nvalid, nused, rowtab, hpk, w_gate_up[0].astype(BF16),
                 w_down[0].astype(BF16), b_gate_up[0][:, None, :], b_down[0][:, None, :], tm, tf)

    gf = g_final[None, :]
    y_prompt = _combine(x1, gate, y, gf, 0, np_, n).reshape(bp, tp, d)
    y_sample = _combine(x1, gate, y, gf, np_, ns, n).reshape(bs, ts, d)
    return (y_prompt, y_sample)
```

```python
import functools

import jax
import jax.numpy as jnp
from jax import lax
from jax.experimental import pallas as pl
from jax.experimental.pallas import tpu as pltpu

GRID_W = 64
GATE_SOFTCAP = 15.0
NA_WIN_ROWS = 8
NA_WIN_COLS = 16
TOP_K = 4
SWIGLU_LIMIT = 7.0
SWIGLU_ALPHA = 1.702
RMS_EPS = 1e-6

LANES = 128
SUBLANES = 8
MLSTM_CHUNK = 256
NA_ROWS_PER_STEP = 16
NEG_BIG = -1e30
MIB = 1024 * 1024

F32 = jnp.float32
BF16 = jnp.bfloat16


def _params(vmem_mib, n_axes):
    return pltpu.CompilerParams(
        dimension_semantics=("arbitrary",) * n_axes, vmem_limit_bytes=vmem_mib * MIB)


def _dot(a, b):
    return jnp.dot(a, b, preferred_element_type=F32)


def _dot_nt(a, b):
    return lax.dot_general(a, b, (((1,), (1,)), ((), ())), preferred_element_type=F32)


def _dot_tn(a, b):
    return lax.dot_general(a, b, (((0,), (0,)), ((), ())), preferred_element_type=F32)


def _split3(x):
    hi = x.astype(BF16)
    r1 = x - hi.astype(F32)
    mid = r1.astype(BF16)
    lo = (r1 - mid.astype(F32)).astype(BF16)
    return hi, mid, lo


def _rms(x, g):
    return x * lax.rsqrt(jnp.mean(x * x, axis=-1, keepdims=True) + RMS_EPS) * g


def _norm_in_kernel(nbp, xp_ref, xs_ref, g_ref, o_ref):
    i = pl.program_id(0)
    x = jnp.where(i < nbp, xp_ref[...], xs_ref[...])
    o_ref[...] = _rms(x, g_ref[...]).astype(o_ref.dtype)


def _norm_in(xp, xs, g, tm=256):
    np_, d = xp.shape
    ns = xs.shape[0]
    nbp, nbs = np_ // tm, ns // tm
    return pl.pallas_call(
        functools.partial(_norm_in_kernel, nbp),
        out_shape=jax.ShapeDtypeStruct((np_ + ns, d), BF16),
        grid=(nbp + nbs,),
        in_specs=[
            pl.BlockSpec((tm, d), lambda i: (jnp.minimum(i, nbp - 1), 0)),
            pl.BlockSpec((tm, d), lambda i: (jnp.maximum(i - nbp, 0), 0)),
            pl.BlockSpec((1, d), lambda i: (0, 0)),
        ],
        out_specs=pl.BlockSpec((tm, d), lambda i: (i, 0)),
        compiler_params=_params(48, 1),
        name="norm_in",
    )(xp, xs, g)


def _matmul_kernel(a_ref, b_ref, o_ref):
    o_ref[...] = _dot(a_ref[...], b_ref[...]).astype(o_ref.dtype)


def _matmul(a, b, tm=1024, tn=1024):
    m, k = a.shape
    n = b.shape[1]
    tm, tn = min(tm, m), min(tn, n)
    return pl.pallas_call(
        _matmul_kernel,
        out_shape=jax.ShapeDtypeStruct((m, n), BF16),
        grid=(m // tm, n // tn),
        in_specs=[pl.BlockSpec((tm, k), lambda i, j: (i, 0)),
                  pl.BlockSpec((k, tn), lambda i, j: (0, j))],
        out_specs=pl.BlockSpec((tm, tn), lambda i, j: (i, j)),
        compiler_params=_params(48, 2),
        name="in_proj",
    )(a, b)


def _softcap(x):
    return GATE_SOFTCAP * jnp.tanh(x / GATE_SOFTCAP)


def _log_sigmoid(x):
    return jnp.minimum(x, 0.0) - jnp.log1p(jnp.exp(-jnp.abs(x)))


def _gates_kernel(L, h_ref, wi_ref, wf_ref, bi_ref, bf_ref, wit_ref, wft_ref, bit_ref, bft_ref,
                  col_ref, row_ref):
    h = h_ref[...]
    t = h.shape[0]
    gi = _softcap(_dot(h, wi_ref[...]) + bi_ref[...])
    lf = _log_sigmoid(_softcap(_dot(h, wf_ref[...]) + bf_ref[...]))
    git = _softcap(_dot_nt(wit_ref[...], h) + bit_ref[...])
    lft = _log_sigmoid(_softcap(_dot_nt(wft_ref[...], h) + bft_ref[...]))

    r = lax.broadcasted_iota(jnp.int32, (L, L), 0)
    c = lax.broadcasted_iota(jnp.int32, (L, L), 1)
    tri = (c <= r).astype(BF16)
    trit = (r <= c).astype(BF16)

    lane = lax.broadcasted_iota(jnp.int32, (L, LANES), 1)
    q = lane % 16
    is_bwd = q >= 3
    kind = jnp.where(is_bwd, q - 3, q)
    rowi = lax.broadcasted_iota(jnp.int32, (16, L), 0)
    row_bwd = (rowi % 2) == 1

    for ci in range(t // L):
        sl = slice(ci * L, (ci + 1) * L)
        lfc, gic = lf[sl], gi[sl]
        hi, mid, lo = _split3(lfc)
        b = _dot(tri, hi) + _dot(tri, mid) + _dot(tri, lo)
        a = b - lfc
        tot = b[L - 1:L, :]
        alpha = jnp.where(is_bwd, -a, b)
        gamma = jnp.where(is_bwd, tot - a, b)
        omega = jnp.where(is_bwd, a + gic, tot - b + gic)
        col_ref[sl, :] = jnp.where(kind == 0, alpha, jnp.where(kind == 1, gamma, omega))

        lfct, gict = lft[:, sl], git[:, sl]
        hi, mid, lo = _split3(lfct)
        bt = _dot(hi, trit) + _dot(mid, trit) + _dot(lo, trit)
        at = bt - lfct
        row_ref[:, sl] = jnp.where(row_bwd, at + gict, gict - bt)


def _gates(h, wi, wf, bi, bf, wit, wft, bit, bft, L, tm=1024):
    n, d = h.shape
    tm = min(tm, n)
    full = lambda i: (0, 0)
    return pl.pallas_call(
        functools.partial(_gates_kernel, L),
        out_shape=(jax.ShapeDtypeStruct((n, LANES), F32), jax.ShapeDtypeStruct((16, n), F32)),
        grid=(n // tm,),
        in_specs=[pl.BlockSpec((tm, d), lambda i: (i, 0)),
                  pl.BlockSpec((d, LANES), full), pl.BlockSpec((d, LANES), full),
                  pl.BlockSpec((1, LANES), full), pl.BlockSpec((1, LANES), full),
                  pl.BlockSpec((16, d), full), pl.BlockSpec((16, d), full),
                  pl.BlockSpec((16, 1), full), pl.BlockSpec((16, 1), full)],
        out_specs=(pl.BlockSpec((tm, LANES), lambda i: (i, 0)),
                   pl.BlockSpec((16, tm), lambda i: (0, i))),
        compiler_params=_params(40, 1),
        name="mlstm_gates",
    )(h, wi, wf, bi, bf, wit, wft, bit, bft)


def _mlstm_kernel(H, L, dk, dv, fwd_ref, bwd_ref, reset_ref,
                  qf_ref, kf_ref, vf_ref, colf_ref, rowf_ref,
                  qb_ref, kb_ref, vb_ref, colb_ref, rowb_ref,
                  hf_ref, hb_ref, ct_ref, nr_ref):
    s = pl.program_id(0)

    @pl.when(reset_ref[s] == 1)
    def _():
        ct_ref[...] = jnp.zeros_like(ct_ref)
        nr_ref[...] = jnp.zeros_like(nr_ref)

    r = lax.broadcasted_iota(jnp.int32, (L, L), 0)
    c = lax.broadcasted_iota(jnp.int32, (L, L), 1)
    masks = (c <= r, c >= r)
    ones = jnp.ones((L, LANES), BF16)
    scale = dk ** -0.5
    dirs = ((qf_ref, kf_ref, vf_ref, colf_ref, rowf_ref, hf_ref),
            (qb_ref, kb_ref, vb_ref, colb_ref, rowb_ref, hb_ref))

    for hd in range(H):
        for d, (q_ref, k_ref, v_ref, col_ref, row_ref, o_ref) in enumerate(dirs):
            ch = hd * 2 + d
            q = q_ref[:, hd * dk:(hd + 1) * dk]
            k = k_ref[:, hd * dk:(hd + 1) * dk]
            v = v_ref[:, hd * dv:(hd + 1) * dv]
            lane0 = hd * 16 + d * 3
            alpha = col_ref[:, lane0:lane0 + 1]
            gamma = col_ref[:, lane0 + 1:lane0 + 2]
            omega = col_ref[:, lane0 + 2:lane0 + 3]
            beta = row_ref[ch:ch + 1, :]
            g_end = gamma[L - 1:L, :] if d == 0 else gamma[0:1, :]

            logd = jnp.where(masks[d], alpha + beta, -jnp.inf)
            p = (_dot_nt(q, k) * scale) * jnp.exp(logd)
            pb = p.astype(BF16)
            ct = ct_ref[ch]
            nr = nr_ref[ch]
            eg = jnp.exp(gamma) * scale
            num = _dot(pb, v) + eg * _dot(q, ct.astype(BF16))
            den = _dot(pb, ones) + eg * _dot(q, nr.astype(BF16))
            inv = 1.0 / jnp.maximum(jnp.abs(den), 1.0)
            o_ref[:, hd * dv:(hd + 1) * dv] = (
                num * jnp.concatenate([inv] * (dv // LANES), axis=1)).astype(o_ref.dtype)

            kw = (k.astype(F32) * jnp.exp(omega)).astype(BF16)
            decay = jnp.exp(g_end)
            ct_ref[ch] = decay * ct + _dot_tn(kw, v)
            nr_ref[ch] = decay * nr + _dot_tn(kw, ones)


def _mlstm(proj, col, row, sched, H, L, dk, dv, q_off, k_off, v_off):
    n = proj.shape[0]
    fwd, bwd, reset = sched
    steps = fwd.shape[0]
    qw, vw = H * dk, H * dv
    qb, kb, vb = q_off // qw, k_off // qw, v_off // vw
    fi = lambda s, f, b, r: f[s]
    bi = lambda s, f, b, r: b[s]

    def specs(pick):
        return [pl.BlockSpec((L, qw), lambda s, f, b, r: (pick(s, f, b, r), qb)),
                pl.BlockSpec((L, qw), lambda s, f, b, r: (pick(s, f, b, r), kb)),
                pl.BlockSpec((L, vw), lambda s, f, b, r: (pick(s, f, b, r), vb)),
                pl.BlockSpec((L, LANES), lambda s, f, b, r: (pick(s, f, b, r), 0)),
                pl.BlockSpec((16, L), lambda s, f, b, r: (0, pick(s, f, b, r)))]

    grid_spec = pltpu.PrefetchScalarGridSpec(
        num_scalar_prefetch=3, grid=(steps,),
        in_specs=specs(fi) + specs(bi),
        out_specs=(pl.BlockSpec((L, vw), lambda s, f, b, r: (f[s], 0)),
                   pl.BlockSpec((L, vw), lambda s, f, b, r: (b[s], 0))),
        scratch_shapes=[pltpu.VMEM((2 * H, dk, dv), F32), pltpu.VMEM((2 * H, dk, LANES), F32)])
    return pl.pallas_call(
        functools.partial(_mlstm_kernel, H, L, dk, dv),
        out_shape=(jax.ShapeDtypeStruct((n, vw), BF16), jax.ShapeDtypeStruct((n, vw), BF16)),
        grid_spec=grid_spec,
        compiler_params=_params(40, 1),
        name="mlstm",
    )(fwd, bwd, reset, proj, proj, proj, col, row, proj, proj, proj, col, row)


def _na_kernel(rows, d, q_ref, k_ref, v_ref, bias_ref, g_ref, o_ref):
    scale = d ** -0.5
    win = NA_WIN_ROWS * GRID_W
    nq = NA_ROWS_PER_STEP * GRID_W
    g = g_ref[...]
    ones = jnp.ones((win, LANES), BF16)

    def block(rb, carry):
        starts, logits = [], []
        for u in range(NA_ROWS_PER_STEP):
            r = rb * NA_ROWS_PER_STEP + u
            rs = jnp.clip(r - NA_WIN_ROWS // 2, 0, rows - NA_WIN_ROWS)
            ks = pl.multiple_of(rs * GRID_W, GRID_W)
            q = q_ref[pl.ds(pl.multiple_of(r * GRID_W, GRID_W), GRID_W), :]
            logits.append(_dot_nt(q, k_ref[pl.ds(ks, win), :]) * scale + bias_ref[r - rs])
            starts.append(ks)
        s = jnp.concatenate(logits, axis=0)
        pb = jnp.exp(s - jnp.max(s, axis=-1, keepdims=True)).astype(BF16)
        l = _dot(pb, ones)
        o = jnp.concatenate(
            [_dot(pb[u * GRID_W:(u + 1) * GRID_W], v_ref[pl.ds(starts[u], win), :])
             for u in range(NA_ROWS_PER_STEP)], axis=0) / l
        o_ref[pl.ds(pl.multiple_of(rb * nq, nq), nq), :] = _rms(o, g).astype(o_ref.dtype)
        return carry

    lax.fori_loop(0, rows // NA_ROWS_PER_STEP, block, 0)


def _na(proj, bias, g, nseq, row0, T, heads, d, q_off, k_off, v_off):
    rows = T // GRID_W
    sb = row0 // T
    qb, kb, vb = q_off // d, k_off // d, v_off // d
    return pl.pallas_call(
        functools.partial(_na_kernel, rows, d),
        out_shape=jax.ShapeDtypeStruct((nseq * T, heads * d), BF16),
        grid=(nseq, heads),
        in_specs=[pl.BlockSpec((T, d), lambda b, h: (sb + b, qb + h)),
                  pl.BlockSpec((T, d), lambda b, h: (sb + b, kb + h)),
                  pl.BlockSpec((T, d), lambda b, h: (sb + b, vb + h)),
                  pl.BlockSpec((None, NA_WIN_ROWS, GRID_W, NA_WIN_ROWS * GRID_W), lambda b, h: (h, 0, 0, 0)),
                  pl.BlockSpec((1, d), lambda b, h: (0, h))],
        out_specs=pl.BlockSpec((T, d), lambda b, h: (b, h)),
        compiler_params=_params(48, 2),
        name="natten",
    )(proj, proj, proj, bias, g)


def _na_bias(rpb):
    heads = rpb.shape[0]
    ncol = 2 * NA_WIN_COLS - 1
    cq = jnp.arange(GRID_W)
    cs = jnp.clip(cq - NA_WIN_COLS // 2, 0, GRID_W - NA_WIN_COLS)
    j = jnp.arange(GRID_W)
    inside = (j[None, :] >= cs[:, None]) & (j[None, :] < cs[:, None] + NA_WIN_COLS)
    coff = j[None, :] - cq[:, None] + (NA_WIN_COLS - 1)
    pick = ((coff[None] == jnp.arange(ncol)[:, None, None]) & inside[None]).astype(F32)
    rsel = jnp.stack([rpb.astype(F32)[:, NA_WIN_ROWS - 1 - var:2 * NA_WIN_ROWS - 1 - var, :]
                      for var in range(NA_WIN_ROWS)], axis=1)
    tab = jnp.einsum('hvic,cqj->hvqij', rsel, pick, precision=lax.Precision.HIGHEST)
    tab = jnp.where(inside[None, None, :, None, :], tab, NEG_BIG)
    return tab.reshape(heads, NA_WIN_ROWS, GRID_W, NA_WIN_ROWS * GRID_W)


def _mlstm_out_kernel(H, dv, hf_ref, hb_ref, om_ref, g_ref, o_ref):
    for hd in range(H):
        sl = slice(hd * dv, (hd + 1) * dv)
        hs = hf_ref[:, sl].astype(F32) + hb_ref[:, sl].astype(F32)
        y = _rms(hs, g_ref[:, sl]) * jax.nn.sigmoid(om_ref[:, sl].astype(F32))
        o_ref[:, sl] = y.astype(o_ref.dtype)


def _mlstm_out(hf, hb, proj, om_off, g_ml, H, dv, tm=512):
    n, wm = hf.shape
    tm = min(tm, n)
    omb = om_off // wm
    row = lambda i: (i, 0)
    return pl.pallas_call(
        functools.partial(_mlstm_out_kernel, H, dv),
        out_shape=jax.ShapeDtypeStruct((n, wm), BF16),
        grid=(n // tm,),
        in_specs=[pl.BlockSpec((tm, wm), row), pl.BlockSpec((tm, wm), row),
                  pl.BlockSpec((tm, wm), lambda i: (i, omb)), pl.BlockSpec((1, wm), lambda i: (0, 0))],
        out_specs=pl.BlockSpec((tm, wm), row),
        compiler_params=_params(40, 1),
        name="mlstm_out",
    )(hf, hb, proj, g_ml)


def _out_proj_kernel(nbp, wm, hm_ref, hap_ref, has_ref, w_ref, xp_ref, xs_ref, o_ref):
    i = pl.program_id(0)
    ha = jnp.where(i < nbp, hap_ref[...], has_ref[...])
    x = jnp.where(i < nbp, xp_ref[...], xs_ref[...])
    o_ref[...] = x + _dot(hm_ref[...], w_ref[:wm, :]) + _dot(ha, w_ref[wm:, :])


def _out_proj(hm, hap, has_, w_out, xp, xs, tm=1024, tn=512):
    n, wm = hm.shape
    wa = hap.shape[1]
    d = w_out.shape[1]
    tm, tn = min(tm, xp.shape[0], xs.shape[0]), min(tn, d)
    nbp = xp.shape[0] // tm
    pidx = lambda i, j: (jnp.minimum(i, nbp - 1), 0)
    sidx = lambda i, j: (jnp.maximum(i - nbp, 0), 0)
    return pl.pallas_call(
        functools.partial(_out_proj_kernel, nbp, wm),
        out_shape=jax.ShapeDtypeStruct((n, d), F32),
        grid=(n // tm, d // tn),
        in_specs=[pl.BlockSpec((tm, wm), lambda i, j: (i, 0)),
                  pl.BlockSpec((tm, wa), pidx),
                  pl.BlockSpec((tm, wa), sidx),
                  pl.BlockSpec((wm + wa, tn), lambda i, j: (0, j)),
                  pl.BlockSpec((tm, tn), lambda i, j: (jnp.minimum(i, nbp - 1), j)),
                  pl.BlockSpec((tm, tn), lambda i, j: (jnp.maximum(i - nbp, 0), j))],
        out_specs=pl.BlockSpec((tm, tn), lambda i, j: (i, j)),
        compiler_params=_params(52, 2),
        name="out_proj",
    )(hm, hap, has_, w_out, xp, xs)


def _pack_bf16_pair(lo, hi):
    lo_bits = lax.bitcast_convert_type(lo.astype(BF16).astype(F32), jnp.uint32)
    hi_bits = lax.bitcast_convert_type(hi.astype(BF16).astype(F32), jnp.uint32)
    return hi_bits | (lo_bits >> 16)


def _unpack_bf16_pair(w):
    lo = lax.bitcast_convert_type(w << 16, F32)
    hi = lax.bitcast_convert_type(w & jnp.uint32(0xFFFF0000), F32)
    return lo, hi


def _router_kernel(E, x_ref, g_ref, whi_ref, wlo_ref, b_ref, idx_ref, gate_ref, rank_ref, cnt_ref, hpk_ref,
                   carry_ref):
    i = pl.program_id(0)

    @pl.when(i == 0)
    def _():
        carry_ref[...] = jnp.zeros_like(carry_ref)

    h = _rms(x_ref[...], g_ref[...])
    tm, d = h.shape
    hpk_ref[...] = _pack_bf16_pair(h[:, :d // 2], h[:, d // 2:])
    hhi = h.astype(BF16)
    hlo = (h - hhi.astype(F32)).astype(BF16)
    whi = whi_ref[...]
    logits = _dot(hhi, whi) + _dot(hlo, whi) + _dot(hhi, wlo_ref[...]) + b_ref[...]

    lane = lax.broadcasted_iota(jnp.int32, (tm, LANES), 1)
    work = jnp.where(lane < E, logits, NEG_BIG)
    vals, idxs = [], []
    for _ in range(TOP_K):
        m = jnp.max(work, axis=-1, keepdims=True)
        am = jnp.min(jnp.where(work == m, lane, LANES), axis=-1, keepdims=True)
        vals.append(m)
        idxs.append(am)
        work = jnp.where(lane == am, -jnp.inf, work)
    es = [jnp.exp(v - vals[0]) for v in vals]
    denom = es[0] + es[1] + es[2] + es[3]

    sel = (work == -jnp.inf).astype(BF16)
    r = lax.broadcasted_iota(jnp.int32, (tm, tm), 0)
    c = lax.broadcasted_iota(jnp.int32, (tm, tm), 1)
    before = _dot((c < r).astype(BF16), sel) + carry_ref[0:1, :]
    carry_ref[...] = carry_ref[...] + jnp.sum(sel.astype(F32), axis=0, keepdims=True)
    cnt_ref[...] = carry_ref[...]

    idx_o = jnp.zeros((tm, LANES), jnp.int32)
    gate_o = jnp.zeros((tm, LANES), F32)
    rank_o = jnp.zeros((tm, LANES), jnp.int32)
    for k in range(TOP_K):
        rk = jnp.sum(jnp.where(lane == idxs[k], before, 0.0), axis=-1, keepdims=True)
        idx_o = jnp.where(lane == k, idxs[k], idx_o)
        gate_o = jnp.where(lane == k, es[k] / denom, gate_o)
        rank_o = jnp.where(lane == k, rk.astype(jnp.int32), rank_o)
    idx_ref[...] = idx_o
    gate_ref[...] = gate_o
    rank_ref[...] = rank_o


def _router(x1, g, whi, wlo, b, E, tm=256):
    n, d = x1.shape
    tm = min(tm, n)
    full = lambda i: (0, 0)
    tok = pl.BlockSpec((tm, LANES), lambda i: (i, 0))
    return pl.pallas_call(
        functools.partial(_router_kernel, E),
        out_shape=(jax.ShapeDtypeStruct((n, LANES), jnp.int32), jax.ShapeDtypeStruct((n, LANES), F32),
                   jax.ShapeDtypeStruct((n, LANES), jnp.int32), jax.ShapeDtypeStruct((8, LANES), F32),
                   jax.ShapeDtypeStruct((n, d // 2), jnp.uint32)),
        grid=(n // tm,),
        in_specs=[pl.BlockSpec((tm, d), lambda i: (i, 0)), pl.BlockSpec((1, d), full),
                  pl.BlockSpec((d, LANES), full), pl.BlockSpec((d, LANES), full),
                  pl.BlockSpec((1, LANES), full)],
        out_specs=(tok, tok, tok, pl.BlockSpec((8, LANES), full), pl.BlockSpec((tm, d // 2), lambda i: (i, 0))),
        scratch_shapes=[pltpu.VMEM((8, LANES), F32)],
        compiler_params=_params(40, 1),
        name="router",
    )(x1, g, whi, wlo, b)


def _expert_kernel(tm, n_tok, nf_static, be_ref, nv_ref, nused_ref, slots_hbm, hpk_hbm, wg_ref, wu_ref, wd_ref,
                   bg_ref, bu_ref, bd_ref, y_hbm, idx_smem, gbuf_ref, xb_ref, acc_ref, ybuf_ref,
                   sem_idx, sem_in, sem_out):
    i = pl.program_id(0)
    f = pl.program_id(1)
    nf = nf_static
    nused = nused_ref[0]
    dh = gbuf_ref.shape[2]
    ngrp = tm // SUBLANES
    spare_row = slots_hbm.shape[0] - 1


    def idx_base(ring):
        return pl.multiple_of(ring * (2 * tm), 2 * tm)

    def idx_fetch(row, ring):
        return pltpu.make_async_copy(slots_hbm.at[row, 0], idx_smem.at[pl.ds(idx_base(ring), 2 * tm)], sem_idx)

    def in_row(tok, gi, u):
        return pltpu.make_async_copy(hpk_hbm.at[pl.ds(tok, 1), :], gbuf_ref.at[gi, pl.ds(u, 1), :], sem_in)

    def gather_start(ring):
        base = idx_base(ring)

        def group(gi, carry):
            for u in range(SUBLANES):
                in_row(idx_smem[base + gi * SUBLANES + u], gi, u).start()
            return carry

        lax.fori_loop(0, ngrp, group, 0)

    def gather_wait():
        def group(gi, carry):
            pltpu.make_async_copy(hpk_hbm.at[pl.ds(0, SUBLANES), :], gbuf_ref.at[0], sem_in).wait()
            return carry

        lax.fori_loop(0, ngrp, group, 0)

    def out_row(gi, u, dst):
        return pltpu.make_async_copy(ybuf_ref.at[gi, pl.ds(u, 1), :], y_hbm.at[pl.ds(dst, 1), :], sem_out)

    def out_group():
        return pltpu.make_async_copy(ybuf_ref.at[0], y_hbm.at[pl.ds(0, SUBLANES), :], sem_out)

    def scatter_wait_all():
        def group(gi, carry):
            out_group().wait()
            return carry

        lax.fori_loop(0, ngrp, group, 0)

    def scatter_start(ring, nv):
        base = idx_base(ring) + tm

        def dst_row(s):
            return idx_smem[base + s]

        def group(gi, carry):
            for u in range(SUBLANES):
                out_row(gi, u, dst_row(gi * SUBLANES + u)).start()
            return carry

        def single(s, carry):
            out_row(lax.shift_right_logical(s, 3), s & (SUBLANES - 1), dst_row(s)).start()
            return carry

        n8 = lax.shift_right_logical(nv, 3)
        lax.fori_loop(0, n8, group, 0)
        lax.fori_loop(n8 * SUBLANES, nv, single, 0)

    def scatter_wait(nv):
        def group(gi, carry):
            out_group().wait()
            return carry

        def single(s, carry):
            out_row(0, 0, 0).wait()
            return carry

        n8 = lax.shift_right_logical(nv, 3)
        lax.fori_loop(0, n8, group, 0)
        lax.fori_loop(n8 * SUBLANES, nv, single, 0)

    @pl.when(i < nused)
    def _():
        nv = nv_ref[i]
        ring_cur = lax.rem(i, 3)
        ring_next = lax.rem(i + 1, 3)
        ring_prev = lax.rem(i + 2, 3)

        @pl.when(f == 0)
        def _():
            @pl.when(i == 0)
            def _():
                for row, ring in ((0, 0), (spare_row, 2)):
                    cp = idx_fetch(row, ring)
                    cp.start()
                    cp.wait()
                gather_start(0)
                ybuf_ref[...] = jnp.zeros_like(ybuf_ref)

            idx_fetch(i + 1, ring_next).start()
            gather_wait()
            lo, hi = _unpack_bf16_pair(gbuf_ref[...].reshape(tm, dh))
            xb_ref[:, :dh] = lo.astype(BF16)
            xb_ref[:, dh:] = hi.astype(BF16)
            acc_ref[...] = jnp.broadcast_to(bd_ref[0], acc_ref.shape)
            idx_fetch(i + 1, ring_next).wait()

        def neighbour_dma_part():
            gbase = idx_base(ring_next)
            sbase = idx_base(ring_prev) + tm
            gps = ngrp // nf
            for j in range(gps):
                gi = f * gps + j
                for u in range(SUBLANES):
                    s = gi * SUBLANES + u
                    in_row(idx_smem[gbase + s], gi, u).start()
                    out_row(gi, u, idx_smem[sbase + s]).start(priority=u % 2)

        def ffn(rows):
            neighbour_dma_part()
            x = xb_ref[:rows, :]
            gg = jnp.minimum(_dot(x, wg_ref[0]) + bg_ref[0], SWIGLU_LIMIT)
            uu = jnp.clip(_dot(x, wu_ref[0]) + bu_ref[0], -SWIGLU_LIMIT, SWIGLU_LIMIT)
            act = ((uu + 1.0) * (gg * jax.nn.sigmoid(gg * SWIGLU_ALPHA))).astype(BF16)
            acc_ref[:rows, :] += _dot(act, wd_ref[0])

        @pl.when(nv > tm // 2)
        def _():
            ffn(tm)

        @pl.when(nv <= tm // 2)
        def _():
            ffn(tm // 2)

        @pl.when(f == nf - 1)
        def _():
            scatter_wait_all()
            y = acc_ref[...]
            ybuf_ref[...] = _pack_bf16_pair(y[:, :dh], y[:, dh:]).reshape(ngrp, SUBLANES, dh)

            @pl.when(i == nused - 1)
            def _():
                scatter_start(ring_cur, nv)
                scatter_wait(nv)
                gather_wait()


def _experts(block_expert, block_nvalid, nused, slots, hpk, wgu, wd, bgu, bd, tm, tf):
    n, dh = hpk.shape
    d = 2 * dh
    E, _, f2 = wgu.shape
    F = f2 // 2
    nb = slots.shape[0] - 1
    nf = F // tf
    assert (tm // SUBLANES) % nf == 0

    def fidx(i, f, nu):
        return jnp.where(i < nu[0], f, nf - 1)

    def bidx(i, be, nu):
        return be[jnp.minimum(i, nu[0] - 1)]

    grid_spec = pltpu.PrefetchScalarGridSpec(
        num_scalar_prefetch=3, grid=(nb, nf),
        in_specs=[pl.BlockSpec(memory_space=pl.ANY), pl.BlockSpec(memory_space=pl.ANY),
                  pl.BlockSpec((1, d, tf), lambda i, f, be, nv, nu: (bidx(i, be, nu), 0, fidx(i, f, nu))),
                  pl.BlockSpec((1, d, tf), lambda i, f, be, nv, nu: (bidx(i, be, nu), 0, nf + fidx(i, f, nu))),
                  pl.BlockSpec((1, tf, d), lambda i, f, be, nv, nu: (bidx(i, be, nu), fidx(i, f, nu), 0)),
                  pl.BlockSpec((1, 1, tf), lambda i, f, be, nv, nu: (bidx(i, be, nu), 0, fidx(i, f, nu))),
                  pl.BlockSpec((1, 1, tf), lambda i, f, be, nv, nu: (bidx(i, be, nu), 0, nf + fidx(i, f, nu))),
                  pl.BlockSpec((1, 1, d), lambda i, f, be, nv, nu: (bidx(i, be, nu), 0, 0))],
        out_specs=pl.BlockSpec(memory_space=pl.ANY),
        scratch_shapes=[pltpu.SMEM((3 * 2 * tm,), jnp.int32), pltpu.VMEM((tm // SUBLANES, SUBLANES, dh), jnp.uint32),
                        pltpu.VMEM((tm, d), BF16), pltpu.VMEM((tm, d), F32), pltpu.VMEM((tm // SUBLANES, SUBLANES, dh), jnp.uint32),
                        pltpu.SemaphoreType.DMA, pltpu.SemaphoreType.DMA, pltpu.SemaphoreType.DMA])
    return pl.pallas_call(
        functools.partial(_expert_kernel, tm, n, nf),
        out_shape=jax.ShapeDtypeStruct((n * TOP_K + tm, dh), jnp.uint32),
        grid_spec=grid_spec,
        compiler_params=_params(56, 2),
        name="experts",
    )(block_expert, block_nvalid, nused, slots, hpk, wgu, wgu, wd, bgu, bgu, bd)


def _combine_kernel(x_ref, gate_ref, y0_ref, y1_ref, y2_ref, y3_ref, g_ref, o_ref):
    d = x_ref.shape[1]
    dh = d // 2
    lo_acc, hi_acc = x_ref[:, :dh], x_ref[:, dh:]
    for k, y_ref in enumerate((y0_ref, y1_ref, y2_ref, y3_ref)):
        lo, hi = _unpack_bf16_pair(y_ref[...])
        gk = gate_ref[:, k:k + 1]
        lo_acc = lo_acc + gk * lo
        hi_acc = hi_acc + gk * hi
    ms = (jnp.sum(lo_acc * lo_acc, axis=-1, keepdims=True) + jnp.sum(hi_acc * hi_acc, axis=-1, keepdims=True)) / d
    r = lax.rsqrt(ms + RMS_EPS)
    o_ref[:, :dh] = lo_acc * r * g_ref[:, :dh]
    o_ref[:, dh:] = hi_acc * r * g_ref[:, dh:]


def _combine(x1, gate, y, g, row0, rows, n_tok, tm=256):
    d = x1.shape[1]
    tm = min(tm, rows)
    b0 = row0 // tm
    nbt = n_tok // tm
    ys = [pl.BlockSpec((tm, d // 2), functools.partial(lambda i, k: (k * nbt + b0 + i, 0), k=k))
          for k in range(TOP_K)]
    return pl.pallas_call(
        _combine_kernel,
        out_shape=jax.ShapeDtypeStruct((rows, d), F32),
        grid=(rows // tm,),
        in_specs=[pl.BlockSpec((tm, d), lambda i: (b0 + i, 0)),
                  pl.BlockSpec((tm, LANES), lambda i: (b0 + i, 0))] + ys +
                 [pl.BlockSpec((1, d), lambda i: (0, 0))],
        out_specs=pl.BlockSpec((tm, d), lambda i: (i, 0)),
        compiler_params=_params(48, 1),
        name="combine",
    )(x1, gate, y, y, y, y, g)


def _gate_weight_tables(w_g, b_g, H):
    d = w_g.shape[0]
    wi = jnp.stack([w_g[:, 0:H], w_g[:, 2 * H:3 * H]], axis=-1)
    wf = jnp.stack([w_g[:, H:2 * H], w_g[:, 3 * H:4 * H]], axis=-1)
    bi = jnp.stack([b_g[0:H], b_g[2 * H:3 * H]], axis=-1)
    bf = jnp.stack([b_g[H:2 * H], b_g[3 * H:4 * H]], axis=-1)

    def col(w):
        w6 = jnp.repeat(w, 3, axis=-1)
        w16 = jnp.pad(w6, [(0, 0)] * (w6.ndim - 1) + [(0, 10)]).reshape(w.shape[:-2] + (H * 16,))
        return jnp.pad(w16, [(0, 0)] * (w16.ndim - 1) + [(0, LANES - H * 16)])

    def rowt(w):
        return jnp.pad(w, ((0, 16 - 2 * H), (0, 0)))

    wit = rowt(wi.reshape(d, 2 * H).T)
    wft = rowt(wf.reshape(d, 2 * H).T)
    return (col(wi).astype(BF16), col(wf).astype(BF16), col(bi)[None].astype(F32), col(bf)[None].astype(F32),
            wit.astype(BF16), wft.astype(BF16), rowt(bi.reshape(2 * H, 1)).astype(F32),
            rowt(bf.reshape(2 * H, 1)).astype(F32))


def _mlstm_schedule(seq_lens, L):
    fwd, bwd, reset, base = [], [], [], 0
    for t in seq_lens:
        nc = t // L
        for c in range(nc):
            fwd.append(base + c)
            bwd.append(base + nc - 1 - c)
            reset.append(1 if c == 0 else 0)
        base += nc
    as_i32 = lambda v: jnp.asarray(v, dtype=jnp.int32)
    return as_i32(fwd), as_i32(bwd), as_i32(reset)


def kernel(x_prompt, x_sample, g_mix, w_in, b_gates, g_mlstm_out, g_na_out, rpb, w_out, g_ffn, w_router,
           b_router, w_gate_up, b_gate_up, w_down, b_down, g_final):
    assert g_mix.shape[0] == 1, "single layer"
    bp, tp, d = x_prompt.shape
    bs, ts, _ = x_sample.shape
    np_, ns = bp * tp, bs * ts
    n = np_ + ns
    H = b_gates.shape[1] // 4
    na_heads = rpb.shape[1]
    wm = g_mlstm_out.shape[1]
    wa = g_na_out.shape[1]
    dv = wm // H
    dk = dv // 2
    da = wa // na_heads
    E = w_router.shape[2]
    F = w_down.shape[2]
    L = min(MLSTM_CHUNK, tp, ts)
    assert H <= 8 and dk == LANES and da == LANES and E <= LANES and np_ % ts == 0

    xp = x_prompt.reshape(np_, d)
    xs = x_sample.reshape(ns, d)

    w = w_in[0]
    o0 = 2 * H * dk + 2 * wm
    w_main = jnp.concatenate([w[:, :o0].astype(BF16), w[:, o0 + 4 * H:].astype(BF16)], axis=1)
    gate_tabs = _gate_weight_tables(w[:, o0:o0 + 4 * H], b_gates[0], H)
    q_off, k_off, v_off, om_off = 0, H * dk, 2 * H * dk, 2 * H * dk + wm
    qa_off, ka_off, va_off = o0, o0 + wa, o0 + 2 * wa

    h = _norm_in(xp, xs, g_mix)
    proj = _matmul(h, w_main)
    col, row = _gates(h, *gate_tabs, L)
    sched = _mlstm_schedule([tp] * bp + [ts] * bs, L)
    hf, hb = _mlstm(proj, col, row, sched, H, L, dk, dv, q_off, k_off, v_off)
    bias = _na_bias(rpb[0])
    hap = _na(proj, bias, g_na_out, bp, 0, tp, na_heads, da, qa_off, ka_off, va_off)
    has_ = _na(proj, bias, g_na_out, bs, np_, ts, na_heads, da, qa_off, ka_off, va_off)
    hm = _mlstm_out(hf, hb, proj, om_off, g_mlstm_out, H, dv)
    x1 = _out_proj(hm, hap, has_, w_out[0].astype(BF16), xp, xs)

    wr = jnp.pad(w_router[0], ((0, 0), (0, LANES - E)))
    wr_hi = wr.astype(BF16)
    wr_lo = (wr - wr_hi.astype(F32)).astype(BF16)
    br = jnp.pad(b_router, ((0, 0), (0, LANES - E)))
    idx, gate, rank, cnt, hpk = _router(x1, g_ffn, wr_hi, wr_lo, br, E)
    idx, rank = idx[:, :TOP_K], rank[:, :TOP_K]

    tm = 512 if (n * TOP_K) % 512 == 0 and n * TOP_K >= 512 * E else 256
    tf = min(512, F)
    nb = n * TOP_K // tm + E
    counts = cnt[0, :E].astype(jnp.int32)
    nblk = (counts + tm - 1) // tm
    pend = jnp.cumsum(nblk) * tm
    pstart = pend - nblk * tm
    dest = (pstart[idx] + rank).reshape(-1)
    slots = jnp.full(((nb + 1) * tm,), -1, jnp.int32).at[dest].set(jnp.arange(n * TOP_K, dtype=jnp.int32),
                                                                 unique_indices=True).reshape(nb + 1, tm)
    real = slots >= 0
    src_row = jnp.where(real, slots >> 2, 0)
    dst_row = jnp.where(real, (slots & (TOP_K - 1)) * n + (slots >> 2),
                        n * TOP_K + jnp.arange(tm, dtype=jnp.int32)[None, :])
    rowtab = jnp.concatenate([src_row, dst_row], axis=1)[:, None, :]
    blk0 = jnp.arange(nb, dtype=jnp.int32) * tm
    block_expert = jnp.minimum(jnp.sum(blk0[:, None] >= pend[None, :], axis=1), E - 1).astype(jnp.int32)
    block_nvalid = jnp.clip(counts[block_expert] - (blk0 - pstart[block_expert]), 0, tm).astype(jnp.int32)
    nused = (pend[-1:] // tm).astype(jnp.int32)

    y = _experts(block_expert, block_nvalid, nused, rowtab, hpk, w_gate_up[0].astype(BF16),
                 w_down[0].astype(BF16), b_gate_up[0][:, None, :], b_down[0][:, None, :], tm, tf)

    gf = g_final[None, :]
    y_prompt = _combine(x1, gate, y, gf, 0, np_, n).reshape(bp, tp, d)
    y_sample = _combine(x1, gate, y, gf, np_, ns, n).reshape(bs, ts, d)
    return (y_prompt, y_sample)
```
